```python
import jax, jax.numpy as jnp
from jax import lax
import numpy as np

D_MODEL = 1024
BATCH = 2
SEQ = 8192
DEPTH = 2
DEC_BATCH = 128
DEC_SEQ = 8
PAST_LEN = 2048
PAGE_SIZE = 128

BRANCH_WIDTH = D_MODEL // 2
N_BRANCH = 3
A_GROUPS = 4
A_GROUP_DIM = BRANCH_WIDTH // A_GROUPS
CHUNK = 128
B_HEADS = 4
B_HEAD_DIM = BRANCH_WIDTH // B_HEADS
MOBA_BLOCK = 256
MOBA_TOPK = 3
Q_CHUNK = 64
ROPE_THETA = 10000.0
C_BLOCKS = 4
C_BLOCK_DIM = BRANCH_WIDTH // C_BLOCKS
CONV_W = 4
LRU_C = 8.0
D_FF = -(-8 * D_MODEL // (3 * 256)) * 256
IN_COLS = 7 * BRANCH_WIDTH + N_BRANCH * D_MODEL
SPLITS = tuple(BRANCH_WIDTH * i for i in range(1, 8))
EPS = 1e-6
F32 = jnp.float32

kernel_name = 'hybrid_gmlp_moba_rglru_decoder_step'


def _normal(key, shape, scale):
    return scale * jax.random.normal(key, shape, F32)


def rmsnorm(x, g):
    xf = x.astype(F32)
    r = lax.rsqrt(jnp.mean(xf * xf, axis=-1, keepdims=True) + EPS)
    return (xf * r).astype(x.dtype) * g


def rope(x, pos):
    half = x.shape[-1] // 2
    inv = ROPE_THETA ** (-jnp.arange(half, dtype=F32) / half)
    ang = pos.astype(F32)[:, None] * inv[None, :]
    cos = jnp.cos(ang)[None, :, None, :]
    sin = jnp.sin(ang)[None, :, None, :]
    x1 = x[..., :half].astype(F32)
    x2 = x[..., half:].astype(F32)
    return jnp.concatenate([x1 * cos - x2 * sin, x2 * cos + x1 * sin], axis=-1).astype(x.dtype)


def chunk_spatial_gate(u, v, w_s, b_s, chunk_len):
    Bn, T, _ = v.shape
    n = chunk_len
    mask = jnp.tril(jnp.ones((n, n), dtype=bool))
    w = jnp.where(mask[None], w_s[:, :n, :n], 0.0)
    vc = v.reshape(Bn, T // n, n, A_GROUPS, A_GROUP_DIM)
    mixed = jnp.einsum('gts,bcsgd->bctgd', w.astype(v.dtype), vc)
    mixed = mixed + b_s[:, :n].T[None, None, :, :, None]
    return u * mixed.reshape(Bn, T, BRANCH_WIDTH)


def moba_attention(q, k_all, v_all, q_pos, q_block):
    Bn, T, H, D = q.shape
    L = k_all.shape[1]
    nb = -(-L // MOBA_BLOCK)
    pad = nb * MOBA_BLOCK - L
    kb = jnp.pad(k_all, ((0, 0), (0, pad), (0, 0), (0, 0))).reshape(Bn, nb, MOBA_BLOCK, H, D)
    vb = jnp.pad(v_all, ((0, 0), (0, pad), (0, 0), (0, 0))).reshape(Bn, nb, MOBA_BLOCK, H, D)
    kmean = jnp.mean(kb, axis=2, dtype=F32)
    n_sel = min(MOBA_TOPK, nb)
    scale = D ** -0.5
    b_idx = jnp.arange(Bn)[:, None, None, None]
    h_idx = jnp.arange(H)[None, :, None, None]
    offs = jnp.arange(MOBA_BLOCK)

    def attend(args):
        qc, pc = args
        own = pc // MOBA_BLOCK
        gate = jnp.einsum('bqhd,bnhd->bhqn', qc.astype(F32), kmean)
        fully_past = jnp.arange(nb)[None, :] < own[:, None]
        gate = jnp.where(fully_past[None, None], gate, -jnp.inf)
        _, sel = lax.top_k(gate, n_sel)
        sel_ok = jnp.arange(n_sel)[None, :] < own[:, None]
        own_b = jnp.broadcast_to(own[None, None, :, None], sel.shape[:-1] + (1,))
        blocks = jnp.concatenate([sel, own_b], axis=-1)
        ok = jnp.concatenate([jnp.broadcast_to(sel_ok[None, None], sel.shape),
                              jnp.ones(own_b.shape, dtype=bool)], axis=-1)
        kg = kb[b_idx, blocks, :, h_idx, :]
        vg = vb[b_idx, blocks, :, h_idx, :]
        logits = jnp.einsum('bqhd,bhqskd->bhqsk', qc, kg, preferred_element_type=F32) * scale
        kpos = blocks[..., None] * MOBA_BLOCK + offs
        mask = ok[..., None] & (kpos <= pc[None, None, :, None, None])
        logits = jnp.where(mask, logits, -jnp.inf)
        shp = logits.shape
        p = jax.nn.softmax(logits.reshape(shp[0], shp[1], shp[2], -1), axis=-1).reshape(shp)
        return jnp.einsum('bhqsk,bhqskd->bqhd', p.astype(vg.dtype), vg)

    nq = T // q_block
    qs = q.reshape(Bn, nq, q_block, H, D).transpose(1, 0, 2, 3, 4)
    ps = q_pos.reshape(nq, q_block)
    out = lax.map(attend, (qs, ps))
    return out.transpose(1, 0, 2, 3, 4).reshape(Bn, T, H, D)


def causal_conv(x, buf, w, b):
    T = x.shape[1]
    xp = jnp.concatenate([buf, x], axis=1)
    y = b + w[0] * xp[:, 0:T]
    for j in range(1, CONV_W):
        y = y + w[j] * xp[:, j:j + T]
    return y, xp[:, -(CONV_W - 1):]


def block_diag(x, w):
    xb = x.reshape(x.shape[0], x.shape[1], C_BLOCKS, C_BLOCK_DIM)
    return jnp.einsum('btgi,gij->btgj', xb, w).reshape(x.shape)


def rg_lru(x, h0, w_a, b_a, w_i, b_i, lam):
    r = jax.nn.sigmoid(block_diag(x, w_a) + b_a).astype(F32)
    i = jax.nn.sigmoid(block_diag(x, w_i) + b_i).astype(F32)
    log_a = -LRU_C * r * jax.nn.softplus(-lam.astype(F32))
    a = jnp.exp(log_a)
    mult = jnp.sqrt(-jnp.expm1(2.0 * log_a))
    xin = x.astype(F32) * i * mult

    def step(h, inp):
        a_t, x_t = inp
        h = a_t * h + x_t
        return h, h

    h_last, hs = lax.scan(step, h0.astype(F32), (jnp.swapaxes(a, 0, 1), jnp.swapaxes(xin, 0, 1)))
    return jnp.swapaxes(hs, 0, 1).astype(x.dtype), h_last.astype(h0.dtype)


def token_mixers(xn, pos, k_past, v_past, h0, conv0, chunk_len, q_block,
                 w_in, a_norm_v, a_w_s, a_b_s, c_conv_w, c_conv_b, c_w_a, c_b_a,
                 c_w_i, c_b_i, c_lambda, w_branch, w_out):
    Bn, T, _ = xn.shape
    z = xn @ w_in
    au, av, zq, zk, zv, cx, cg, zg = jnp.split(z, SPLITS, axis=-1)
    u = jax.nn.gelu(au)
    v = rmsnorm(jax.nn.gelu(av), a_norm_v)
    y_a = chunk_spatial_gate(u, v, a_w_s, a_b_s, chunk_len)
    q = rope(zq.reshape(Bn, T, B_HEADS, B_HEAD_DIM), pos)
    k = rope(zk.reshape(Bn, T, B_HEADS, B_HEAD_DIM), pos)
    vv = zv.reshape(Bn, T, B_HEADS, B_HEAD_DIM)
    if k_past is None:
        k_all, v_all = k, vv
    else:
        k_all = jnp.concatenate([k_past, k], axis=1)
        v_all = jnp.concatenate([v_past, vv], axis=1)
    y_b = moba_attention(q, k_all, v_all, pos, q_block).reshape(Bn, T, BRANCH_WIDTH)
    xc, conv_new = causal_conv(cx, conv0, c_conv_w, c_conv_b)
    hs, h_new = rg_lru(xc, h0, c_w_a, c_b_a, c_w_i, c_b_i, c_lambda)
    y_c = hs * jax.nn.gelu(cg)
    ys = jnp.stack([y_a, y_b, y_c], axis=2)
    proj = jnp.einsum('btnw,nwd->btnd', ys, w_branch)
    gates = jax.nn.sigmoid(zg.reshape(Bn, T, N_BRANCH, D_MODEL))
    merged = jnp.sum(gates * proj, axis=2)
    return merged @ w_out, k, vv, h_new, conv_new, v[:, -chunk_len:]


def swiglu(x, w_gu, w_down):
    g, u = jnp.split(x @ w_gu, 2, axis=-1)
    return (jax.nn.silu(g) * u) @ w_down


def layer(l, x, pos, k_past, v_past, h0, conv0, chunk_len, q_block, W):
    xn = rmsnorm(x, W['norm_mix_pre'][l])
    mix, k, v, h_new, conv_new, chunk_v = token_mixers(
        xn, pos, k_past, v_past, h0, conv0, chunk_len, q_block,
        W['w_in'][l], W['a_norm_v'][l], W['a_w_s'][l], W['a_b_s'][l],
        W['c_conv_w'][l], W['c_conv_b'][l], W['c_w_a'][l], W['c_b_a'][l],
        W['c_w_i'][l], W['c_b_i'][l], W['c_lambda'][l], W['w_branch'][l], W['w_out'][l])
    x = x + rmsnorm(mix, W['norm_mix_post'][l])
    f = swiglu(rmsnorm(x, W['norm_ffn_pre'][l]), W['ffn_w_gu'][l], W['ffn_w_down'][l])
    x = x + rmsnorm(f, W['norm_ffn_post'][l])
    return x, k, v, h_new, conv_new, chunk_v


def setup_inputs(seed: int = 0) -> dict:
    key = jax.random.key(seed)
    ks = jax.random.split(key, 26)
    n_pages = PAST_LEN // PAGE_SIZE
    n_phys = (DEC_BATCH * n_pages * 5) // 4
    W = BRANCH_WIDTH
    perm = jax.random.permutation(ks[4], n_phys)
    page_table = perm[:DEC_BATCH * n_pages].reshape(DEC_BATCH, n_pages).astype(jnp.int32)
    u = jax.random.uniform(ks[21], (DEPTH, W), F32, 0.9, 0.999)
    a = u ** (1.0 / LRU_C)
    c_lambda = jnp.log(a) - jnp.log1p(-a)
    return {
        'x_prompt': _normal(ks[0], (BATCH, SEQ, D_MODEL), 1.0),
        'x_sample': _normal(ks[1], (DEC_BATCH, DEC_SEQ, D_MODEL), 1.0),
        'cache_k': _normal(ks[2], (DEPTH, n_phys, PAGE_SIZE, B_HEADS, B_HEAD_DIM), 1.0),
        'cache_v': _normal(ks[3], (DEPTH, n_phys, PAGE_SIZE, B_HEADS, B_HEAD_DIM), 1.0),
        'page_table': page_table,
        'state_lru_h': _normal(ks[5], (DEPTH, DEC_BATCH, W), 0.5),
        'state_conv': _normal(ks[6], (DEPTH, DEC_BATCH, CONV_W - 1, W), 1.0),
        'norm_mix_pre': 1.0 + _normal(ks[7], (DEPTH, D_MODEL), 0.1),
        'norm_mix_post': 1.0 + _normal(ks[8], (DEPTH, D_MODEL), 0.1),
        'norm_ffn_pre': 1.0 + _normal(ks[9], (DEPTH, D_MODEL), 0.1),
        'norm_ffn_post': 1.0 + _normal(ks[10], (DEPTH, D_MODEL), 0.1),
        'w_in': _normal(ks[11], (DEPTH, D_MODEL, IN_COLS), D_MODEL ** -0.5),
        'a_norm_v': 1.0 + _normal(ks[12], (DEPTH, W), 0.1),
        'a_w_s': _normal(ks[13], (DEPTH, A_GROUPS, CHUNK, CHUNK), CHUNK ** -0.5),
        'a_b_s': 1.0 + _normal(ks[14], (DEPTH, A_GROUPS, CHUNK), 0.1),
        'c_conv_w': _normal(ks[15], (DEPTH, CONV_W, W), CONV_W ** -0.5),
        'c_conv_b': _normal(ks[16], (DEPTH, W), 0.01),
        'c_w_a': _normal(ks[17], (DEPTH, C_BLOCKS, C_BLOCK_DIM, C_BLOCK_DIM), C_BLOCK_DIM ** -0.5),
        'c_b_a': _normal(ks[18], (DEPTH, W), 0.1),
        'c_w_i': _normal(ks[19], (DEPTH, C_BLOCKS, C_BLOCK_DIM, C_BLOCK_DIM), C_BLOCK_DIM ** -0.5),
        'c_b_i': _normal(ks[20], (DEPTH, W), 0.1),
        'c_lambda': c_lambda,
        'w_branch': _normal(ks[22], (DEPTH, N_BRANCH, W, D_MODEL), W ** -0.5),
        'w_out': _normal(ks[23], (DEPTH, D_MODEL, D_MODEL), D_MODEL ** -0.5),
        'ffn_w_gu': _normal(ks[24], (DEPTH, D_MODEL, 2 * D_FF), D_MODEL ** -0.5),
        'ffn_w_down': _normal(ks[25], (DEPTH, D_FF, D_MODEL), D_FF ** -0.5),
    }


def reference(x_prompt, x_sample, cache_k, cache_v, page_table, state_lru_h, state_conv,
              norm_mix_pre, norm_mix_post, norm_ffn_pre, norm_ffn_post, w_in, a_norm_v,
              a_w_s, a_b_s, c_conv_w, c_conv_b, c_w_a, c_b_a, c_w_i, c_b_i, c_lambda,
              w_branch, w_out, ffn_w_gu, ffn_w_down):
    W = {'norm_mix_pre': norm_mix_pre, 'norm_mix_post': norm_mix_post,
         'norm_ffn_pre': norm_ffn_pre, 'norm_ffn_post': norm_ffn_post,
         'w_in': w_in, 'a_norm_v': a_norm_v, 'a_w_s': a_w_s, 'a_b_s': a_b_s,
         'c_conv_w': c_conv_w, 'c_conv_b': c_conv_b, 'c_w_a': c_w_a, 'c_b_a': c_b_a,
         'c_w_i': c_w_i, 'c_b_i': c_b_i, 'c_lambda': c_lambda,
         'w_branch': w_branch, 'w_out': w_out, 'ffn_w_gu': ffn_w_gu, 'ffn_w_down': ffn_w_down}
    Bp, Tp = x_prompt.shape[0], x_prompt.shape[1]
    Bs, Ts = x_sample.shape[0], x_sample.shape[1]
    n_pages = page_table.shape[1]
    pos_p = jnp.arange(Tp, dtype=jnp.int32)
    pos_s = PAST_LEN + jnp.arange(Ts, dtype=jnp.int32)
    chunk_s = min(Ts, CHUNK)
    y_prompt, y_sample = x_prompt, x_sample
    kp, vp, ksm, vsm, hp, hsm, cp, csm, ap, asm = [], [], [], [], [], [], [], [], [], []
    for l in range(DEPTH):
        h0p = jnp.zeros((Bp, BRANCH_WIDTH), x_prompt.dtype)
        conv0p = jnp.zeros((Bp, CONV_W - 1, BRANCH_WIDTH), x_prompt.dtype)
        y_prompt, k1, v1, h1, c1, a1 = layer(l, y_prompt, pos_p, None, None, h0p, conv0p,
                                             CHUNK, Q_CHUNK, W)
        k_past = cache_k[l][page_table].reshape(Bs, n_pages * PAGE_SIZE, B_HEADS, B_HEAD_DIM)
        v_past = cache_v[l][page_table].reshape(Bs, n_pages * PAGE_SIZE, B_HEADS, B_HEAD_DIM)
        y_sample, k2, v2, h2, c2, a2 = layer(l, y_sample, pos_s, k_past, v_past, state_lru_h[l],
                                             state_conv[l], chunk_s, 1, W)
        kp.append(k1); vp.append(v1); hp.append(h1); cp.append(c1); ap.append(a1)
        ksm.append(k2); vsm.append(v2); hsm.append(h2); csm.append(c2); asm.append(a2)
    return (y_prompt, y_sample, jnp.stack(kp), jnp.stack(vp), jnp.stack(ksm), jnp.stack(vsm),
            jnp.stack(hp), jnp.stack(hsm), jnp.stack(cp), jnp.stack(csm), jnp.stack(ap), jnp.stack(asm))
```

```python
import functools

import jax
import jax.numpy as jnp
from jax import lax
from jax.experimental import pallas as pl
from jax.experimental.pallas import tpu as pltpu

F32 = jnp.float32
BF16 = jnp.bfloat16

EPS = 1e-6
ROPE_THETA = 10000.0
LRU_C = 8.0
HEADS = 4
HEAD_DIM = 128
CHUNK = 128
MOBA_BLOCK = 256
MOBA_TOPK = 3
CONV_W = 4
NEG = -1e30
LANES = 128
SUBLANES = 8
VMEM_LIMIT = 56 * 1024 * 1024

_NT = (((1,), (1,)), ((), ()))


def _cparams(*sem):
    return pltpu.CompilerParams(dimension_semantics=sem, vmem_limit_bytes=VMEM_LIMIT)


def _const_spec(shape):
    zeros = (0,) * len(shape)
    return pl.BlockSpec(shape, lambda *_: zeros)


def _rms(x):
    return x * lax.rsqrt(jnp.mean(x * x, axis=-1, keepdims=True) + EPS)


def _rope_table_kernel(inv_ref, cos_ref, sin_ref, *, pos0, period):
    n = cos_ref.shape[0]
    row = lax.broadcasted_iota(jnp.int32, (n, HEAD_DIM), 0)
    lane = lax.broadcasted_iota(jnp.int32, (n, HEAD_DIM), 1)
    pos = (pos0 + row % period).astype(F32)
    ang = pos * inv_ref[...]
    s = jnp.sin(ang)
    cos_ref[...] = jnp.cos(ang)
    sin_ref[...] = jnp.where(lane < HEAD_DIM // 2, -s, s)


def _rope_tables(n_rows, pos0, period):
    half = HEAD_DIM // 2
    inv = ROPE_THETA ** (-jnp.arange(half, dtype=F32) / half)
    inv = jnp.concatenate([inv, inv]).reshape(1, HEAD_DIM)
    return pl.pallas_call(
        functools.partial(_rope_table_kernel, pos0=pos0, period=period),
        out_shape=(jax.ShapeDtypeStruct((n_rows, HEAD_DIM), F32),) * 2,
        name="rope_tables",
    )(inv)


def _rope(z, cos, sin):
    outs = []
    for h in range(HEADS):
        zh = z[:, h * HEAD_DIM:(h + 1) * HEAD_DIM]
        outs.append(zh * cos + pltpu.roll(zh, HEAD_DIM // 2, 1) * sin)
    return jnp.concatenate(outs, axis=-1)


def _inproj_kernel(x_ref, g_ref, w_ref, gv_ref, cos_ref, sin_ref,
                   u_ref, v_ref, q_ref, k_ref, vv_ref, cx_ref, cg_ref, *, width):
    xn = (_rms(x_ref[...]) * g_ref[...]).astype(BF16)

    def proj(c):
        return jnp.dot(xn, w_ref[:, c * width:(c + 1) * width], preferred_element_type=F32)

    u_ref[...] = jax.nn.gelu(proj(0))
    v_ref[...] = _rms(jax.nn.gelu(proj(1))) * gv_ref[...]
    cos = cos_ref[...]
    sin = sin_ref[...]
    q_ref[...] = _rope(proj(2), cos, sin)
    k_ref[...] = _rope(proj(3), cos, sin)
    vv_ref[...] = proj(4)
    cx_ref[...] = proj(5)
    cg_ref[...] = jax.nn.gelu(proj(6))


def _inproj(x, g, w_mix, gv, cos, sin, *, tm, table_tiles):
    n, d = x.shape
    width = w_mix.shape[1] // 7
    row = lambda i: (i, 0)
    tab = lambda i: (i % table_tiles, 0)
    out = jax.ShapeDtypeStruct((n, width), F32)
    return pl.pallas_call(
        functools.partial(_inproj_kernel, width=width),
        grid=(n // tm,),
        in_specs=[pl.BlockSpec((tm, d), row), _const_spec((1, d)), _const_spec(w_mix.shape),
                  _const_spec((1, width)), pl.BlockSpec((tm, HEAD_DIM), tab),
                  pl.BlockSpec((tm, HEAD_DIM), tab)],
        out_specs=[pl.BlockSpec((tm, width), row)] * 7,
        out_shape=(out,) * 7,
        compiler_params=_cparams("parallel"),
        name="inproj",
    )(x, g, w_mix, gv, cos, sin)


def _softplus(x):
    return jnp.maximum(x, 0.0) + jnp.log1p(jnp.exp(-jnp.abs(x)))


def _lanes(g):
    return slice(g * HEAD_DIM, (g + 1) * HEAD_DIM)


def _lru_coeffs(xc, g, wa_ref, ba_ref, wi_ref, bi_ref, lam_ref):
    xb = xc.astype(BF16)
    r = jax.nn.sigmoid(jnp.dot(xb, wa_ref[g], preferred_element_type=F32) + ba_ref[:, _lanes(g)])
    gate_i = jax.nn.sigmoid(jnp.dot(xb, wi_ref[g], preferred_element_type=F32) + bi_ref[:, _lanes(g)])
    log_a = (-LRU_C) * r * _softplus(-lam_ref[:, _lanes(g)])
    a = jnp.exp(log_a)
    th = jnp.tanh(log_a)
    mult = jnp.sqrt(-2.0 * th / (1.0 - th))
    return a, xc * gate_i * mult


SCAN_ROWS = SUBLANES * SUBLANES


def _mix_prompt_kernel(u_ref, v_ref, cx_ref, cg_ref, ws_ref, bias_ref, cw_ref, cb_ref,
                       wa_ref, ba_ref, wi_ref, bi_ref, lam_ref,
                       ya_ref, yc_ref, h_ref,
                       xbuf, a_scr, x_scr, hs_scr, h_scr, *, tt):
    t_idx = pl.program_id(1)

    @pl.when(t_idx == 0)
    def _():
        xbuf[0:SUBLANES, :] = jnp.zeros((SUBLANES, xbuf.shape[1]), F32)
        h_scr[...] = jnp.zeros(h_scr.shape, F32)

    r_i = lax.broadcasted_iota(jnp.int32, (CHUNK, CHUNK), 0)
    c_i = lax.broadcasted_iota(jnp.int32, (CHUNK, CHUNK), 1)
    tril = c_i <= r_i
    w_low = [jnp.where(tril, ws_ref[g], 0.0).astype(BF16) for g in range(HEADS)]
    for c in range(tt // CHUNK):
        rows = slice(c * CHUNK, (c + 1) * CHUNK)
        vc = v_ref[0, rows, :].astype(BF16)
        mixed = jnp.concatenate(
            [jnp.dot(w_low[g], vc[:, g * HEAD_DIM:(g + 1) * HEAD_DIM], preferred_element_type=F32)
             for g in range(HEADS)], axis=-1)
        ya_ref[0, rows, :] = u_ref[0, rows, :] * (mixed + bias_ref[...])

    xbuf[SUBLANES:SUBLANES + tt, :] = cx_ref[0]
    xc = cb_ref[...] + cw_ref[CONV_W - 1:CONV_W, :] * xbuf[SUBLANES:SUBLANES + tt, :]
    for d in range(1, CONV_W):
        xc = xc + cw_ref[CONV_W - 1 - d:CONV_W - d, :] * xbuf[SUBLANES - d:SUBLANES - d + tt, :]
    xbuf[0:SUBLANES, :] = xbuf[tt:tt + SUBLANES, :]

    for g in range(HEADS):
        a, xin = _lru_coeffs(xc[:, _lanes(g)], g, wa_ref, ba_ref, wi_ref, bi_ref, lam_ref)
        a_scr[g] = a
        x_scr[g] = xin
        h = h_scr[0:1, _lanes(g)]
        for s in range(tt // SCAN_ROWS):
            base = s * SCAN_ROWS
            p_run, s_run = [], []
            for j in range(SUBLANES):
                a_j = a_scr[g, pl.ds(base + j, SUBLANES, stride=SUBLANES), :]
                x_j = x_scr[g, pl.ds(base + j, SUBLANES, stride=SUBLANES), :]
                if j == 0:
                    p_run.append(a_j)
                    s_run.append(x_j)
                else:
                    p_run.append(p_run[-1] * a_j)
                    s_run.append(a_j * s_run[-1] + x_j)
            carry_in = []
            for m in range(SUBLANES):
                carry_in.append(h)
                h = s_run[-1][m:m + 1, :] + p_run[-1][m:m + 1, :] * h
            cin = jnp.concatenate(carry_in, axis=0)
            for j in range(SUBLANES):
                hs_scr[g, pl.ds(base + j, SUBLANES, stride=SUBLANES), :] = s_run[j] + p_run[j] * cin
        h_scr[:, _lanes(g)] = jnp.broadcast_to(h, (SUBLANES, HEAD_DIM))
    h_ref[0] = h_scr[...]
    yc_ref[0] = jnp.concatenate([hs_scr[g] for g in range(HEADS)], axis=-1) * cg_ref[0]


def _mix_prompt(u, v, cx, cg, ws, bias_full, cw, cb, wa, ba, wi, bi, lam, *, tt):
    b, t, w = u.shape
    seq = pl.BlockSpec((1, tt, w), lambda i, j: (i, j, 0))
    ins = [u, v, cx, cg, ws, bias_full, cw, cb, wa, ba, wi, bi, lam]
    slab = pltpu.VMEM((HEADS, tt, HEAD_DIM), F32)
    return pl.pallas_call(
        functools.partial(_mix_prompt_kernel, tt=tt),
        grid=(b, t // tt),
        in_specs=[seq] * 4 + [_const_spec(a.shape) for a in ins[4:]],
        out_specs=[seq, seq, pl.BlockSpec((1, SUBLANES, w), lambda i, j: (i, 0, 0))],
        out_shape=(jax.ShapeDtypeStruct((b, t, w), F32), jax.ShapeDtypeStruct((b, t, w), F32),
                   jax.ShapeDtypeStruct((b, SUBLANES, w), F32)),
        scratch_shapes=[pltpu.VMEM((tt + SUBLANES, w), F32), slab, slab, slab,
                        pltpu.VMEM((SUBLANES, w), F32)],
        compiler_params=_cparams("arbitrary", "arbitrary"),
        name="mix_prompt",
    )(*ins)


def _mix_sample_kernel(u_ref, v_ref, cx_ref, cg_ref, wrow_ref, brow_ref, cw_ref, cb_ref,
                       wa_ref, ba_ref, wi_ref, bi_ref, lam_ref, h0_ref, conv0_ref,
                       ya_ref, yc_ref, h_ref, in_scr, out_scr, *, steps, nseq):
    w = u_ref.shape[1]
    for n, ref in enumerate((u_ref, v_ref, cx_ref, cg_ref)):
        for g in range(HEADS):
            in_scr[n * HEADS + g] = ref[:, _lanes(g)]

    def at_step(n, g, t):
        return in_scr[n * HEADS + g, pl.ds(t, nseq, stride=steps), :]

    for g in range(HEADS):
        v_t = [at_step(1, g, t) for t in range(steps)]
        for t in range(steps):
            mixed = brow_ref[t:t + 1, _lanes(g)]
            for s in range(t + 1):
                mixed = mixed + wrow_ref[t * steps + s:t * steps + s + 1, _lanes(g)] * v_t[s]
            out_scr[g, pl.ds(t, nseq, stride=steps), :] = at_step(0, g, t) * mixed

        xp = [conv0_ref[:, j * w + g * HEAD_DIM:j * w + (g + 1) * HEAD_DIM] for j in range(CONV_W - 1)]
        xp += [at_step(2, g, t) for t in range(steps)]
        h = h0_ref[:, _lanes(g)]
        for t in range(steps):
            xc = cb_ref[:, _lanes(g)]
            for j in range(CONV_W):
                xc = xc + cw_ref[j:j + 1, _lanes(g)] * xp[t + j]
            a, xin = _lru_coeffs(xc, g, wa_ref, ba_ref, wi_ref, bi_ref, lam_ref)
            h = a * h + xin
            out_scr[HEADS + g, pl.ds(t, nseq, stride=steps), :] = h * at_step(3, g, t)
        h_ref[:, _lanes(g)] = h
    ya_ref[...] = jnp.concatenate([out_scr[g] for g in range(HEADS)], axis=-1)
    yc_ref[...] = jnp.concatenate([out_scr[HEADS + g] for g in range(HEADS)], axis=-1)


def _mix_sample(u, v, cx, cg, wrow, brow, cw, cb, wa, ba, wi, bi, lam, h0, conv0, *, steps):
    n, w = u.shape
    nseq = n // steps
    ins = [u, v, cx, cg, wrow, brow, cw, cb, wa, ba, wi, bi, lam, h0, conv0]
    return pl.pallas_call(
        functools.partial(_mix_sample_kernel, steps=steps, nseq=nseq),
        grid=(1,),
        in_specs=[_const_spec(a.shape) for a in ins],
        out_specs=[_const_spec((n, w)), _const_spec((n, w)), _const_spec((nseq, w))],
        out_shape=(jax.ShapeDtypeStruct((n, w), F32), jax.ShapeDtypeStruct((n, w), F32),
                   jax.ShapeDtypeStruct((nseq, w), F32)),
        scratch_shapes=[pltpu.VMEM((4 * HEADS, n, HEAD_DIM), F32),
                        pltpu.VMEM((2 * HEADS, n, HEAD_DIM), F32)],
        compiler_params=_cparams("arbitrary"),
        name="mix_sample",
    )(*ins)


def _topk_mask(g, idx, n, axis):
    sel = jnp.zeros(g.shape, jnp.bool_)
    for _ in range(MOBA_TOPK):
        best = jnp.max(g, axis=axis, keepdims=True)
        first = jnp.min(jnp.where(g == best, idx, n), axis=axis, keepdims=True)
        pick = idx == first
        sel = jnp.logical_or(sel, pick)
        g = jnp.where(pick, -jnp.inf, g)
    return sel


def _attn_prompt_kernel(q_ref, k_ref, v_ref, o_ref, kmean_scr, kb_scr, vb_scr, *, nb):
    blk = MOBA_BLOCK
    i = pl.program_id(2)

    @pl.when(i == 0)
    def _():
        for j in range(nb):
            kj = k_ref[0, j * blk:(j + 1) * blk, :]
            kmean_scr[j:j + 1, :] = jnp.sum(kj, axis=0, keepdims=True) * (1.0 / blk)
            kb_scr[j * blk:(j + 1) * blk, :] = kj.astype(BF16)
            vb_scr[j * blk:(j + 1) * blk, :] = v_ref[0, j * blk:(j + 1) * blk, :].astype(BF16)

    q = q_ref[0]
    gate = lax.dot_general(q, kmean_scr[...], _NT, precision=lax.Precision.HIGHEST,
                           preferred_element_type=F32)
    col = lax.broadcasted_iota(jnp.int32, (blk, nb), 1)
    past = col < i
    sel = jnp.logical_and(_topk_mask(jnp.where(past, gate, -jnp.inf), col, nb, 1), past)
    sel_bias = jnp.where(sel, 0.0, NEG)

    qs = (q * (HEAD_DIM ** -0.5)).astype(BF16)
    own = pl.multiple_of(i * blk, blk)
    s = lax.dot_general(qs, kb_scr[pl.ds(own, blk), :], _NT, preferred_element_type=F32)
    r_i = lax.broadcasted_iota(jnp.int32, (blk, blk), 0)
    c_i = lax.broadcasted_iota(jnp.int32, (blk, blk), 1)
    s = jnp.where(c_i <= r_i, s, NEG)
    m0 = jnp.max(s, axis=1, keepdims=True)
    p = jnp.exp(s - m0)
    l0 = jnp.sum(p, axis=1, keepdims=True)
    acc0 = jnp.dot(p.astype(BF16), vb_scr[pl.ds(own, blk), :], preferred_element_type=F32)

    def body(j, carry):
        m, l, acc = carry
        start = pl.multiple_of(j * blk, blk)
        bias_j = jnp.sum(jnp.where(col == j, sel_bias, 0.0), axis=1, keepdims=True)
        s = lax.dot_general(qs, kb_scr[pl.ds(start, blk), :], _NT, preferred_element_type=F32) + bias_j
        m_new = jnp.maximum(m, jnp.max(s, axis=1, keepdims=True))
        alpha = jnp.exp(m - m_new)
        p = jnp.exp(s - m_new)
        l = alpha * l + jnp.sum(p, axis=1, keepdims=True)
        acc = alpha * acc + jnp.dot(p.astype(BF16), vb_scr[pl.ds(start, blk), :],
                                    preferred_element_type=F32)
        return m_new, l, acc

    _, l, acc = lax.fori_loop(0, i, body, (m0, l0, acc0))
    o_ref[0] = acc / l


def _attn_prompt(q, k, v):
    b, t, w = q.shape
    nb = t // MOBA_BLOCK
    qspec = pl.BlockSpec((1, MOBA_BLOCK, HEAD_DIM), lambda bi, h, i: (bi, i, h))
    kvspec = pl.BlockSpec((1, t, HEAD_DIM), lambda bi, h, i: (bi, 0, h))
    return pl.pallas_call(
        functools.partial(_attn_prompt_kernel, nb=nb),
        grid=(b, HEADS, nb),
        in_specs=[qspec, kvspec, kvspec],
        out_specs=qspec,
        out_shape=jax.ShapeDtypeStruct((b, t, w), F32),
        scratch_shapes=[pltpu.VMEM((nb, HEAD_DIM), F32), pltpu.VMEM((t, HEAD_DIM), BF16),
                        pltpu.VMEM((t, HEAD_DIM), BF16)],
        compiler_params=_cparams("arbitrary", "arbitrary", "arbitrary"),
        name="attn_prompt",
    )(q, k, v)


def _attn_sample_kernel(pt_ref, q_ref, kn_ref, vn_ref, *rest, n_pages, page, steps):
    del pt_ref
    kp = rest[:n_pages]
    vp = rest[n_pages:2 * n_pages]
    o_ref, kbuf, vbuf = rest[2 * n_pages:]
    w = q_ref.shape[2]
    past = n_pages * page
    nbp = past // MOBA_BLOCK
    per_blk = MOBA_BLOCK // page
    cols = LANES

    sums = []
    for p in range(n_pages):
        kpg = kp[p][0, 0]
        sums.append(jnp.sum(kpg, axis=0, keepdims=True))
        kbuf[p * page:(p + 1) * page, :] = kpg.astype(BF16)
        vbuf[p * page:(p + 1) * page, :] = vp[p][0, 0].astype(BF16)
    pad = jnp.zeros((LANES - steps, w), F32)
    kbuf[past:past + LANES, :] = jnp.concatenate([kn_ref[0], pad], axis=0).astype(BF16)
    vbuf[past:past + LANES, :] = jnp.concatenate([vn_ref[0], pad], axis=0).astype(BF16)
    kmean = jnp.concatenate(
        [sum(sums[j * per_blk:(j + 1) * per_blk]) for j in range(nbp)], axis=0) * (1.0 / MOBA_BLOCK)

    q = q_ref[0]
    r_i = lax.broadcasted_iota(jnp.int32, (cols, w), 0)
    l_i = lax.broadcasted_iota(jnp.int32, (cols, w), 1)
    q_rep = jnp.concatenate([q] * (cols // steps), axis=0)
    qmat = jnp.where(r_i // steps == l_i // HEAD_DIM, q_rep, 0.0)

    gate = lax.dot_general(kmean, qmat, _NT, precision=lax.Precision.HIGHEST,
                           preferred_element_type=F32)
    blk_i = lax.broadcasted_iota(jnp.int32, (nbp, cols), 0)
    sel_bias = jnp.where(_topk_mask(gate, blk_i, nbp, 0), 0.0, NEG)

    qs = (qmat * (HEAD_DIM ** -0.5)).astype(BF16)
    s = lax.dot_general(kbuf[...], qs, _NT, preferred_element_type=F32)
    s_past = (s[:past].reshape(nbp, MOBA_BLOCK, cols) + sel_bias[:, None, :]).reshape(past, cols)
    key_i = lax.broadcasted_iota(jnp.int32, (LANES, cols), 0)
    col_i = lax.broadcasted_iota(jnp.int32, (LANES, cols), 1)
    s_own = jnp.where(key_i <= col_i % steps, s[past:], NEG)
    m = jnp.maximum(jnp.max(s_past, axis=0, keepdims=True), jnp.max(s_own, axis=0, keepdims=True))
    p_t = jnp.concatenate([jnp.exp(s_past - m), jnp.exp(s_own - m)], axis=0).T
    l = jnp.sum(p_t, axis=1, keepdims=True)
    out = jnp.dot(p_t.astype(BF16), vbuf[...], preferred_element_type=F32) / l
    o_ref[0] = jnp.concatenate(
        [out[h * steps:(h + 1) * steps, h * HEAD_DIM:(h + 1) * HEAD_DIM] for h in range(HEADS)],
        axis=-1)


def _attn_sample(page_table, q, k_new, v_new, cache_k, cache_v, layer):
    bs, steps, w = q.shape
    n_pages = page_table.shape[1]
    page = cache_k.shape[2]
    new = pl.BlockSpec((1, steps, w), lambda b, pt: (b, 0, 0))

    def page_spec(p):
        return pl.BlockSpec((1, 1, page, w), lambda b, pt: (layer, pt[b, p], 0, 0))

    keys = n_pages * page + LANES
    return pl.pallas_call(
        functools.partial(_attn_sample_kernel, n_pages=n_pages, page=page, steps=steps),
        grid_spec=pltpu.PrefetchScalarGridSpec(
            num_scalar_prefetch=1,
            grid=(bs,),
            in_specs=[new, new, new] + [page_spec(p) for p in range(n_pages)] * 2,
            out_specs=new,
            scratch_shapes=[pltpu.VMEM((keys, w), BF16), pltpu.VMEM((keys, w), BF16)],
        ),
        out_shape=jax.ShapeDtypeStruct((bs, steps, w), F32),
        compiler_params=_cparams("arbitrary"),
        name="attn_sample",
    )(page_table, q, k_new, v_new, *([cache_k] * n_pages), *([cache_v] * n_pages))


def _merge_kernel(x_ref, ya_ref, yb_ref, yc_ref, gpre_ref, wg_ref, wb_ref, wo_ref, gpost_ref,
                  o_ref, *, d):
    x = x_ref[...]
    xn = (_rms(x) * gpre_ref[...]).astype(BF16)
    merged = None
    for n, y_ref in enumerate((ya_ref, yb_ref, yc_ref)):
        gate = jax.nn.sigmoid(jnp.dot(xn, wg_ref[:, n * d:(n + 1) * d], preferred_element_type=F32))
        proj = jnp.dot(y_ref[...].astype(BF16), wb_ref[n], preferred_element_type=F32)
        merged = gate * proj if merged is None else merged + gate * proj
    mix = jnp.dot(merged.astype(BF16), wo_ref[...], preferred_element_type=F32)
    o_ref[...] = x + _rms(mix) * gpost_ref[...]


def _merge(x, ya, yb, yc, gpre, wg, wb, wo, gpost, *, tm):
    n, d = x.shape
    w = ya.shape[1]
    row = lambda i: (i, 0)
    return pl.pallas_call(
        functools.partial(_merge_kernel, d=d),
        grid=(n // tm,),
        in_specs=[pl.BlockSpec((tm, d), row)] + [pl.BlockSpec((tm, w), row)] * 3
                 + [_const_spec(a.shape) for a in (gpre, wg, wb, wo, gpost)],
        out_specs=pl.BlockSpec((tm, d), row),
        out_shape=jax.ShapeDtypeStruct((n, d), F32),
        compiler_params=_cparams("parallel"),
        name="merge",
    )(x, ya, yb, yc, gpre, wg, wb, wo, gpost)


def _ffn_kernel(x_ref, gpre_ref, wgu_ref, wd_ref, gpost_ref, o_ref, *, d_ff, n_split):
    x = x_ref[...]
    xn = (_rms(x) * gpre_ref[...]).astype(BF16)
    cw = d_ff // n_split
    f = None
    for c in range(n_split):
        g = jnp.dot(xn, wgu_ref[:, c * cw:(c + 1) * cw], preferred_element_type=F32)
        u = jnp.dot(xn, wgu_ref[:, d_ff + c * cw:d_ff + (c + 1) * cw], preferred_element_type=F32)
        act = (jax.nn.silu(g) * u).astype(BF16)
        part = jnp.dot(act, wd_ref[c * cw:(c + 1) * cw, :], preferred_element_type=F32)
        f = part if f is None else f + part
    o_ref[...] = x + _rms(f) * gpost_ref[...]


def _ffn(x, gpre, wgu, wd, gpost, *, tm):
    n, d = x.shape
    d_ff = wd.shape[0]
    n_split = 2 if (d_ff // 2) % LANES == 0 else 1
    row = lambda i: (i, 0)
    return pl.pallas_call(
        functools.partial(_ffn_kernel, d_ff=d_ff, n_split=n_split),
        grid=(n // tm,),
        in_specs=[pl.BlockSpec((tm, d), row)] + [_const_spec(a.shape) for a in (gpre, wgu, wd, gpost)],
        out_specs=pl.BlockSpec((tm, d), row),
        out_shape=jax.ShapeDtypeStruct((n, d), F32),
        compiler_params=_cparams("parallel"),
        name="ffn",
    )(x, gpre, wgu, wd, gpost)


def _row_tile(n):
    for tm in (256, 128, 64, 32, 16, 8):
        if n % tm == 0:
            return tm
    raise ValueError(f"row count {n} is not a multiple of {SUBLANES}")


def _layer_weights(l, P):
    width = P['a_norm_v'].shape[1]
    row = lambda a: a[l].reshape(1, -1)
    w_in = P['w_in'][l].astype(BF16)
    return dict(
        g_mix_pre=row(P['norm_mix_pre']), g_mix_post=row(P['norm_mix_post']),
        g_ffn_pre=row(P['norm_ffn_pre']), g_ffn_post=row(P['norm_ffn_post']),
        w_mix=w_in[:, :7 * width], w_gate=w_in[:, 7 * width:], gv=row(P['a_norm_v']),
        ws=P['a_w_s'][l], bs=P['a_b_s'][l],
        cw=P['c_conv_w'][l], cb=row(P['c_conv_b']),
        wa=P['c_w_a'][l].astype(BF16), ba=row(P['c_b_a']),
        wi=P['c_w_i'][l].astype(BF16), bi=row(P['c_b_i']), lam=row(P['c_lambda']),
        wb=P['w_branch'][l].astype(BF16), wo=P['w_out'][l].astype(BF16),
        wgu=P['ffn_w_gu'][l].astype(BF16), wd=P['ffn_w_down'][l].astype(BF16))


def _finish_layer(x, ya, yb, yc, L, tm):
    x1 = _merge(x, ya, yb, yc, L['g_mix_pre'], L['w_gate'], L['wb'], L['wo'], L['g_mix_post'], tm=tm)
    return _ffn(x1, L['g_ffn_pre'], L['wgu'], L['wd'], L['g_ffn_post'], tm=tm)


def _prompt_layer(x, L, cos, sin):
    b, t, d = x.shape
    width = L['gv'].shape[1]
    tm = _row_tile(t)
    xf = x.reshape(b * t, d)
    u, v, q, k, vv, cx, cg = _inproj(xf, L['g_mix_pre'], L['w_mix'], L['gv'], cos, sin,
                                     tm=tm, table_tiles=t // tm)
    seq = lambda a: a.reshape(b, t, width)
    bias_full = jnp.repeat(L['bs'][:, :CHUNK].T, HEAD_DIM, axis=1)
    ya, yc, h_last = _mix_prompt(seq(u), seq(v), seq(cx), seq(cg), L['ws'][:, :CHUNK, :CHUNK], bias_full,
                                 L['cw'], L['cb'], L['wa'], L['ba'], L['wi'], L['bi'], L['lam'],
                                 tt=_row_tile(t))
    yb = _attn_prompt(seq(q), seq(k), seq(vv))
    y = _finish_layer(xf, ya.reshape(b * t, width), yb.reshape(b * t, width),
                      yc.reshape(b * t, width), L, tm)
    heads = lambda a: a.reshape(b, t, HEADS, HEAD_DIM)
    return (y.reshape(b, t, d), heads(k), heads(vv), h_last[:, 0],
            seq(cx)[:, t - (CONV_W - 1):], seq(v)[:, t - CHUNK:])


def _sample_layer(l, x, L, cos, sin, page_table, cache_k, cache_v, h0, conv0):
    bs, steps, d = x.shape
    width = L['gv'].shape[1]
    n = bs * steps
    tm = _row_tile(n)
    xf = x.reshape(n, d)
    u, v, q, k, vv, cx, cg = _inproj(xf, L['g_mix_pre'], L['w_mix'], L['gv'], cos, sin,
                                     tm=tm, table_tiles=1)
    w_low = jnp.tril(L['ws'][:, :steps, :steps])
    wrow = jnp.repeat(w_low.transpose(1, 2, 0).reshape(steps * steps, HEADS), HEAD_DIM, axis=1)
    brow = jnp.repeat(L['bs'][:, :steps].T, HEAD_DIM, axis=1)
    ya, yc, h_new = _mix_sample(u, v, cx, cg, wrow, brow, L['cw'], L['cb'], L['wa'], L['ba'],
                                L['wi'], L['bi'], L['lam'], h0, conv0.reshape(bs, -1), steps=steps)
    seq = lambda a: a.reshape(bs, steps, width)
    yb = _attn_sample(page_table, seq(q), seq(k), seq(vv), cache_k, cache_v, l)
    y = _finish_layer(xf, ya, yb.reshape(n, width), yc, L, tm)
    conv_new = jnp.concatenate([conv0, seq(cx)], axis=1)[:, -(CONV_W - 1):]
    heads = lambda a: a.reshape(bs, steps, HEADS, HEAD_DIM)
    return y.reshape(bs, steps, d), heads(k), heads(vv), h_new, conv_new, seq(v)


def kernel(x_prompt, x_sample, cache_k, cache_v, page_table, state_lru_h, state_conv, norm_mix_pre, norm_mix_post, norm_ffn_pre, norm_ffn_post, w_in, a_norm_v, a_w_s, a_b_s, c_conv_w, c_conv_b, c_w_a, c_b_a, c_w_i, c_b_i, c_lambda, w_branch, w_out, ffn_w_gu, ffn_w_down):
    P = dict(norm_mix_pre=norm_mix_pre, norm_mix_post=norm_mix_post, norm_ffn_pre=norm_ffn_pre,
             norm_ffn_post=norm_ffn_post, w_in=w_in, a_norm_v=a_norm_v, a_w_s=a_w_s, a_b_s=a_b_s,
             c_conv_w=c_conv_w, c_conv_b=c_conv_b, c_w_a=c_w_a, c_b_a=c_b_a, c_w_i=c_w_i,
             c_b_i=c_b_i, c_lambda=c_lambda, w_branch=w_branch, w_out=w_out,
             ffn_w_gu=ffn_w_gu, ffn_w_down=ffn_w_down)
    depth = w_in.shape[0]
    tp = x_prompt.shape[1]
    bs, steps, _ = x_sample.shape
    n_pages, page = page_table.shape[1], cache_k.shape[2]
    past = n_pages * page
    ck = cache_k.reshape(cache_k.shape[0], cache_k.shape[1], page, HEADS * HEAD_DIM)
    cv = cache_v.reshape(ck.shape)

    cos_p, sin_p = _rope_tables(tp, 0, tp)
    cos_s, sin_s = _rope_tables(_row_tile(bs * steps), past, steps)

    yp, ys = x_prompt, x_sample
    outs = [[] for _ in range(10)]
    for l in range(depth):
        L = _layer_weights(l, P)
        yp, k1, v1, h1, c1, a1 = _prompt_layer(yp, L, cos_p, sin_p)
        ys, k2, v2, h2, c2, a2 = _sample_layer(l, ys, L, cos_s, sin_s, page_table, ck, cv,
                                               state_lru_h[l], state_conv[l])
        for lst, val in zip(outs, (k1, v1, k2, v2, h1, h2, c1, c2, a1, a2)):
            lst.append(val)
    return (yp, ys) + tuple(jnp.stack(o) for o in outs)
```

```python
import functools

import jax
import jax.numpy as jnp
from jax import lax
from jax.experimental import pallas as pl
from jax.experimental.pallas import tpu as pltpu

F32 = jnp.float32
BF16 = jnp.bfloat16

EPS = 1e-6
ROPE_THETA = 10000.0
LRU_C = 8.0
HEADS = 4
HEAD_DIM = 128
CHUNK = 128
MOBA_BLOCK = 256
MOBA_TOPK = 3
CONV_W = 4
NEG = -1e30
LOG2_E = 1.4426950408889634
LANES = 128
SUBLANES = 8
VMEM_LIMIT = 56 * 1024 * 1024

_NT = (((1,), (1,)), ((), ()))


def _cparams(*sem):
    return pltpu.CompilerParams(dimension_semantics=sem, vmem_limit_bytes=VMEM_LIMIT)


def _const_spec(shape):
    zeros = (0,) * len(shape)
    return pl.BlockSpec(shape, lambda *_: zeros)


def _rms(x):
    return x * lax.rsqrt(jnp.mean(x * x, axis=-1, keepdims=True) + EPS)


def _rope_table_kernel(inv_ref, cos_ref, sin_ref, *, pos0, period):
    n = cos_ref.shape[0]
    row = lax.broadcasted_iota(jnp.int32, (n, HEAD_DIM), 0)
    lane = lax.broadcasted_iota(jnp.int32, (n, HEAD_DIM), 1)
    pos = (pos0 + row % period).astype(F32)
    ang = pos * inv_ref[...]
    s = jnp.sin(ang)
    cos_ref[...] = jnp.cos(ang)
    sin_ref[...] = jnp.where(lane < HEAD_DIM // 2, -s, s)


def _rope_tables(n_rows, pos0, period):
    half = HEAD_DIM // 2
    inv = ROPE_THETA ** (-jnp.arange(half, dtype=F32) / half)
    inv = jnp.concatenate([inv, inv]).reshape(1, HEAD_DIM)
    return pl.pallas_call(
        functools.partial(_rope_table_kernel, pos0=pos0, period=period),
        out_shape=(jax.ShapeDtypeStruct((n_rows, HEAD_DIM), F32),) * 2,
        name="rope_tables",
    )(inv)


def _rope(z, cos, sin):
    outs = []
    for h in range(HEADS):
        zh = z[:, h * HEAD_DIM:(h + 1) * HEAD_DIM]
        outs.append(zh * cos + pltpu.roll(zh, HEAD_DIM // 2, 1) * sin)
    return jnp.concatenate(outs, axis=-1)


def _lanes(g):
    return slice(g * HEAD_DIM, (g + 1) * HEAD_DIM)


def _store_head_rows(ref, z, tm):
    for h in range(HEADS):
        ref[pl.ds(h, tm, stride=HEADS), :] = z[:, _lanes(h)]


def _inproj_kernel(x_ref, g_ref, w_ref, gv_ref, cos_ref, sin_ref, *outs, width, prompt):
    xn = (_rms(x_ref[...]) * g_ref[...]).astype(BF16)
    tm = x_ref.shape[0]

    def proj(c):
        return jnp.dot(xn, w_ref[:, c * width:(c + 1) * width], preferred_element_type=F32)

    if prompt:
        u_ref, v_ref, cx_ref, cg_ref, ko_ref, vo_ref, kb_ref, qt_ref, vt_ref, km_ref = outs
    else:
        u_ref, v_ref, cx_ref, cg_ref, q_ref, k_ref, vv_ref = outs
    u_ref[...] = jax.nn.gelu(proj(0))
    v_ref[...] = _rms(jax.nn.gelu(proj(1))) * gv_ref[...]
    cx_ref[...] = proj(5)
    cg_ref[...] = jax.nn.gelu(proj(6))
    cos = cos_ref[...]
    sin = sin_ref[...]
    q = _rope(proj(2), cos, sin)
    k = _rope(proj(3), cos, sin)
    vv = proj(4)
    if prompt:
        qt_ref[...] = q.T
        vt_ref[...] = vv.T.astype(BF16)
        kb_ref[...] = k.astype(BF16)
        for j in range(tm // MOBA_BLOCK):
            kj = k[j * MOBA_BLOCK:(j + 1) * MOBA_BLOCK]
            km_ref[j] = jnp.sum(kj, axis=0, keepdims=True) * (1.0 / MOBA_BLOCK)
        _store_head_rows(ko_ref, k, tm)
        _store_head_rows(vo_ref, vv, tm)
    else:
        q_ref[...] = q
        k_ref[...] = k
        vv_ref[...] = vv


def _inproj(x, g, w_mix, gv, cos, sin, *, tm, table_tiles, prompt):
    n, d = x.shape
    width = w_mix.shape[1] // 7
    row = lambda i: (i, 0)
    col = lambda i: (0, i)
    tab = lambda i: (i % table_tiles, 0)
    wide = (jax.ShapeDtypeStruct((n, width), F32), pl.BlockSpec((tm, width), row))
    if prompt:
        heads = (jax.ShapeDtypeStruct((n * HEADS, HEAD_DIM), F32), pl.BlockSpec((tm * HEADS, HEAD_DIM), row))
        nblk = tm // MOBA_BLOCK
        outs = [wide] * 4 + [heads] * 2 + [
            (jax.ShapeDtypeStruct((n, width), BF16), pl.BlockSpec((tm, width), row)),
            (jax.ShapeDtypeStruct((width, n), F32), pl.BlockSpec((width, tm), col)),
            (jax.ShapeDtypeStruct((width, n), BF16), pl.BlockSpec((width, tm), col)),
            (jax.ShapeDtypeStruct((n // MOBA_BLOCK, 1, width), F32),
             pl.BlockSpec((nblk, 1, width), lambda i: (i, 0, 0)))]
    else:
        outs = [wide] * 7
    return pl.pallas_call(
        functools.partial(_inproj_kernel, width=width, prompt=prompt),
        grid=(n // tm,),
        in_specs=[pl.BlockSpec((tm, d), row), _const_spec((1, d)), _const_spec(w_mix.shape),
                  _const_spec((1, width)), pl.BlockSpec((tm, HEAD_DIM), tab),
                  pl.BlockSpec((tm, HEAD_DIM), tab)],
        out_specs=[o[1] for o in outs],
        out_shape=tuple(o[0] for o in outs),
        compiler_params=_cparams("parallel"),
        name="inproj_prompt" if prompt else "inproj_sample",
    )(x, g, w_mix, gv, cos, sin)


def _softplus(x):
    return jnp.maximum(x, 0.0) + jnp.log1p(jnp.exp(-jnp.abs(x)))


def _lru_coeffs(xc, g, wa_ref, ba_ref, wi_ref, bi_ref, lam_ref):
    xb = xc.astype(BF16)
    r = jax.nn.sigmoid(jnp.dot(xb, wa_ref[g], preferred_element_type=F32) + ba_ref[:, _lanes(g)])
    gate_i = jax.nn.sigmoid(jnp.dot(xb, wi_ref[g], preferred_element_type=F32) + bi_ref[:, _lanes(g)])
    log_a = (-LRU_C) * r * _softplus(-lam_ref[:, _lanes(g)])
    a = jnp.exp(log_a)
    th = jnp.tanh(log_a)
    mult = jnp.sqrt(-2.0 * th / (1.0 - th))
    return a, xc * gate_i * mult


SCAN_ROWS = SUBLANES * SUBLANES


def _mix_prompt_kernel(u_ref, v_ref, cx_ref, cg_ref, ws_ref, bias_ref, cw_ref, cb_ref,
                       wa_ref, ba_ref, wi_ref, bi_ref, lam_ref,
                       ya_ref, yc_ref, h_ref,
                       xbuf, a_scr, x_scr, hs_scr, h_scr, *, tt):
    t_idx = pl.program_id(1)

    @pl.when(t_idx == 0)
    def _():
        xbuf[0:SUBLANES, :] = jnp.zeros((SUBLANES, xbuf.shape[1]), F32)
        h_scr[...] = jnp.zeros(h_scr.shape, F32)

    r_i = lax.broadcasted_iota(jnp.int32, (CHUNK, CHUNK), 0)
    c_i = lax.broadcasted_iota(jnp.int32, (CHUNK, CHUNK), 1)
    tril = c_i <= r_i
    w_low = [jnp.where(tril, ws_ref[g], 0.0).astype(BF16) for g in range(HEADS)]
    for c in range(tt // CHUNK):
        rows = slice(c * CHUNK, (c + 1) * CHUNK)
        vc = v_ref[0, rows, :].astype(BF16)
        mixed = jnp.concatenate(
            [jnp.dot(w_low[g], vc[:, g * HEAD_DIM:(g + 1) * HEAD_DIM], preferred_element_type=F32)
             for g in range(HEADS)], axis=-1)
        ya_ref[0, rows, :] = u_ref[0, rows, :] * (mixed + bias_ref[...])

    xbuf[SUBLANES:SUBLANES + tt, :] = cx_ref[0]
    xc = cb_ref[...] + cw_ref[CONV_W - 1:CONV_W, :] * xbuf[SUBLANES:SUBLANES + tt, :]
    for d in range(1, CONV_W):
        xc = xc + cw_ref[CONV_W - 1 - d:CONV_W - d, :] * xbuf[SUBLANES - d:SUBLANES - d + tt, :]
    xbuf[0:SUBLANES, :] = xbuf[tt:tt + SUBLANES, :]

    for g in range(HEADS):
        a, xin = _lru_coeffs(xc[:, _lanes(g)], g, wa_ref, ba_ref, wi_ref, bi_ref, lam_ref)
        a_scr[g] = a
        x_scr[g] = xin
        h = h_scr[0:1, _lanes(g)]
        for s in range(tt // SCAN_ROWS):
            base = s * SCAN_ROWS
            p_run, s_run = [], []
            for j in range(SUBLANES):
                a_j = a_scr[g, pl.ds(base + j, SUBLANES, stride=SUBLANES), :]
                x_j = x_scr[g, pl.ds(base + j, SUBLANES, stride=SUBLANES), :]
                if j == 0:
                    p_run.append(a_j)
                    s_run.append(x_j)
                else:
                    p_run.append(p_run[-1] * a_j)
                    s_run.append(a_j * s_run[-1] + x_j)
            carry_in = []
            for m in range(SUBLANES):
                carry_in.append(h)
                h = s_run[-1][m:m + 1, :] + p_run[-1][m:m + 1, :] * h
            cin = jnp.concatenate(carry_in, axis=0)
            for j in range(SUBLANES):
                hs_scr[g, pl.ds(base + j, SUBLANES, stride=SUBLANES), :] = s_run[j] + p_run[j] * cin
        h_scr[:, _lanes(g)] = jnp.broadcast_to(h, (SUBLANES, HEAD_DIM))
    h_ref[0] = h_scr[...]
    yc_ref[0] = jnp.concatenate([hs_scr[g] for g in range(HEADS)], axis=-1) * cg_ref[0]


def _mix_prompt(u, v, cx, cg, ws, bias_full, cw, cb, wa, ba, wi, bi, lam, *, tt):
    b, t, w = u.shape
    seq = pl.BlockSpec((1, tt, w), lambda i, j: (i, j, 0))
    ins = [u, v, cx, cg, ws, bias_full, cw, cb, wa, ba, wi, bi, lam]
    slab = pltpu.VMEM((HEADS, tt, HEAD_DIM), F32)
    return pl.pallas_call(
        functools.partial(_mix_prompt_kernel, tt=tt),
        grid=(b, t // tt),
        in_specs=[seq] * 4 + [_const_spec(a.shape) for a in ins[4:]],
        out_specs=[seq, seq, pl.BlockSpec((1, SUBLANES, w), lambda i, j: (i, 0, 0))],
        out_shape=(jax.ShapeDtypeStruct((b, t, w), F32), jax.ShapeDtypeStruct((b, t, w), F32),
                   jax.ShapeDtypeStruct((b, SUBLANES, w), F32)),
        scratch_shapes=[pltpu.VMEM((tt + SUBLANES, w), F32), slab, slab, slab,
                        pltpu.VMEM((SUBLANES, w), F32)],
        compiler_params=_cparams("arbitrary", "arbitrary"),
        name="mix_prompt",
    )(*ins)


def _mix_sample_kernel(u_ref, v_ref, cx_ref, cg_ref, wrow_ref, brow_ref, cw_ref, cb_ref,
                       wa_ref, ba_ref, wi_ref, bi_ref, lam_ref, h0_ref, conv0_ref,
                       ya_ref, yc_ref, h_ref, in_scr, out_scr, *, steps, nseq):
    w = u_ref.shape[1]
    for n, ref in enumerate((u_ref, v_ref, cx_ref, cg_ref)):
        for g in range(HEADS):
            in_scr[n * HEADS + g] = ref[:, _lanes(g)]

    def at_step(n, g, t):
        return in_scr[n * HEADS + g, pl.ds(t, nseq, stride=steps), :]

    for g in range(HEADS):
        v_t = [at_step(1, g, t) for t in range(steps)]
        for t in range(steps):
            mixed = brow_ref[t:t + 1, _lanes(g)]
            for s in range(t + 1):
                mixed = mixed + wrow_ref[t * steps + s:t * steps + s + 1, _lanes(g)] * v_t[s]
            out_scr[g, pl.ds(t, nseq, stride=steps), :] = at_step(0, g, t) * mixed

        xp = [conv0_ref[:, j * w + g * HEAD_DIM:j * w + (g + 1) * HEAD_DIM] for j in range(CONV_W - 1)]
        xp += [at_step(2, g, t) for t in range(steps)]
        h = h0_ref[:, _lanes(g)]
        for t in range(steps):
            xc = cb_ref[:, _lanes(g)]
            for j in range(CONV_W):
                xc = xc + cw_ref[j:j + 1, _lanes(g)] * xp[t + j]
            a, xin = _lru_coeffs(xc, g, wa_ref, ba_ref, wi_ref, bi_ref, lam_ref)
            h = a * h + xin
            out_scr[HEADS + g, pl.ds(t, nseq, stride=steps), :] = h * at_step(3, g, t)
        h_ref[:, _lanes(g)] = h
    ya_ref[...] = jnp.concatenate([out_scr[g] for g in range(HEADS)], axis=-1)
    yc_ref[...] = jnp.concatenate([out_scr[HEADS + g] for g in range(HEADS)], axis=-1)


def _mix_sample(u, v, cx, cg, wrow, brow, cw, cb, wa, ba, wi, bi, lam, h0, conv0, *, steps):
    n, w = u.shape
    nseq = n // steps
    ins = [u, v, cx, cg, wrow, brow, cw, cb, wa, ba, wi, bi, lam, h0, conv0]
    return pl.pallas_call(
        functools.partial(_mix_sample_kernel, steps=steps, nseq=nseq),
        grid=(1,),
        in_specs=[_const_spec(a.shape) for a in ins],
        out_specs=[_const_spec((n, w)), _const_spec((n, w)), _const_spec((nseq, w))],
        out_shape=(jax.ShapeDtypeStruct((n, w), F32), jax.ShapeDtypeStruct((n, w), F32),
                   jax.ShapeDtypeStruct((nseq, w), F32)),
        scratch_shapes=[pltpu.VMEM((4 * HEADS, n, HEAD_DIM), F32),
                        pltpu.VMEM((2 * HEADS, n, HEAD_DIM), F32)],
        compiler_params=_cparams("arbitrary"),
        name="mix_sample",
    )(*ins)


def _topk_mask(g, idx, n, axis):
    sel = jnp.zeros(g.shape, jnp.bool_)
    for _ in range(MOBA_TOPK):
        best = jnp.max(g, axis=axis, keepdims=True)
        first = jnp.min(jnp.where(g == best, idx, n), axis=axis, keepdims=True)
        pick = idx == first
        sel = jnp.logical_or(sel, pick)
        g = jnp.where(pick, -jnp.inf, g)
    return sel


SUM_ROWS = 16


def _attn_prompt_kernel(qt_ref, kb_ref, vt_ref, km_ref, o_ref, bias_scr, *, nb, group):
    blk = MOBA_BLOCK
    span = group * blk
    i = pl.program_id(1)
    own = pl.multiple_of(i * blk, blk)
    blk_i = lax.broadcasted_iota(jnp.int32, (nb, blk), 0)
    past = blk_i < i
    key_i = lax.broadcasted_iota(jnp.int32, (blk, blk), 0)
    qry_i = lax.broadcasted_iota(jnp.int32, (blk, blk), 1)

    def values(h, start, size):
        ones = jnp.ones((SUM_ROWS, size), BF16)
        return jnp.concatenate([vt_ref[_lanes(h), pl.ds(start, size)], ones], axis=0)

    qs, m0, acc0 = [], [], []
    for h in range(HEADS):
        qt = qt_ref[_lanes(h), :]
        gate = jnp.dot(km_ref[0, :, _lanes(h)], qt, precision=lax.Precision.HIGHEST,
                       preferred_element_type=F32)
        sel = jnp.logical_and(_topk_mask(jnp.where(past, gate, -jnp.inf), blk_i, nb, 0), past)
        sel_bias = jnp.where(sel, 0.0, NEG)
        for sj in range(nb // group):
            bias_scr[h, sj, 0:group, :] = sel_bias[sj * group:(sj + 1) * group, :]
        qs.append((qt * (HEAD_DIM ** -0.5 * LOG2_E)).astype(BF16))
        s = jnp.dot(kb_ref[0, pl.ds(own, blk), _lanes(h)], qs[h], preferred_element_type=F32)
        s = jnp.where(key_i <= qry_i, s, NEG)
        m0.append(jnp.max(s, axis=0, keepdims=True))
        acc0.append(jnp.dot(values(h, own, blk), jnp.exp2(s - m0[h]).astype(BF16),
                            preferred_element_type=F32))

    def body(sj, carry):
        start = pl.multiple_of(sj * span, span)
        out = []
        for h in range(HEADS):
            m, acc = carry[h]
            s = jnp.dot(kb_ref[0, pl.ds(start, span), _lanes(h)], qs[h], preferred_element_type=F32)
            bias = bias_scr[h, sj, 0:group, :]
            s = (s.reshape(group, blk, blk) + bias[:, None, :]).reshape(span, blk)
            m_new = jnp.maximum(m, jnp.max(s, axis=0, keepdims=True))
            p = jnp.exp2(s - m_new).astype(BF16)
            acc = jnp.exp2(m - m_new) * acc + jnp.dot(values(h, start, span), p,
                                                      preferred_element_type=F32)
            out.append((m_new, acc))
        return tuple(out)

    fin = lax.fori_loop(0, (i + group - 1) // group, body, tuple(zip(m0, acc0)))
    o_ref[0] = jnp.concatenate(
        [(acc[:HEAD_DIM] / acc[HEAD_DIM:HEAD_DIM + 1]).T for _, acc in fin], axis=-1)


def _attn_prompt(qt, kb, vt, kmean):
    b, t, w = kb.shape
    nb = t // MOBA_BLOCK
    group = 4 if nb % 4 == 0 else 1
    return pl.pallas_call(
        functools.partial(_attn_prompt_kernel, nb=nb, group=group),
        grid=(b, nb),
        in_specs=[pl.BlockSpec((w, MOBA_BLOCK), lambda bi, i: (0, bi * nb + i)),
                  pl.BlockSpec((1, t, w), lambda bi, i: (bi, 0, 0)),
                  pl.BlockSpec((w, t), lambda bi, i: (0, bi)),
                  pl.BlockSpec((1, nb, w), lambda bi, i: (bi, 0, 0))],
        out_specs=pl.BlockSpec((1, MOBA_BLOCK, w), lambda bi, i: (bi, i, 0)),
        out_shape=jax.ShapeDtypeStruct((b, t, w), F32),
        scratch_shapes=[pltpu.VMEM((HEADS, nb // group, SUBLANES, MOBA_BLOCK), F32)],
        compiler_params=_cparams("arbitrary", "arbitrary"),
        name="attn_prompt",
    )(qt, kb, vt, kmean)


def _attn_sample_kernel(pt_ref, q_ref, kn_ref, vn_ref, *rest, n_pages, page, steps):
    del pt_ref
    kp = rest[:n_pages]
    vp = rest[n_pages:2 * n_pages]
    o_ref, kbuf, vbuf = rest[2 * n_pages:]
    past = n_pages * page
    nbp = past // MOBA_BLOCK
    per_blk = MOBA_BLOCK // page
    cols = LANES

    pad = jnp.zeros((LANES - steps, HEAD_DIM), F32)
    kmean = []
    for h in range(HEADS):
        sums = []
        for p in range(n_pages):
            kph = kp[p][0, 0, pl.ds(h, page, stride=HEADS), :]
            sums.append(jnp.sum(kph, axis=0, keepdims=True))
            kbuf[h, p * page:(p + 1) * page, :] = kph.astype(BF16)
            vbuf[h, p * page:(p + 1) * page, :] = vp[p][0, 0, pl.ds(h, page, stride=HEADS), :].astype(BF16)
        kbuf[h, past:past + LANES, :] = jnp.concatenate([kn_ref[0][:, _lanes(h)], pad], axis=0).astype(BF16)
        vbuf[h, past:past + LANES, :] = jnp.concatenate([vn_ref[0][:, _lanes(h)], pad], axis=0).astype(BF16)
        kmean.append(jnp.concatenate(
            [sum(sums[j * per_blk:(j + 1) * per_blk]) for j in range(nbp)], axis=0) * (1.0 / MOBA_BLOCK))

    q = q_ref[0]
    r_i = lax.broadcasted_iota(jnp.int32, (cols, HEAD_DIM), 0)
    gate = None
    s = None
    for h in range(HEADS):
        q_rep = jnp.concatenate([q[:, _lanes(h)]] * (cols // steps), axis=0)
        qmat = jnp.where(r_i // steps == h, q_rep, 0.0)
        g_h = lax.dot_general(kmean[h], qmat, _NT, precision=lax.Precision.HIGHEST,
                              preferred_element_type=F32)
        s_h = lax.dot_general(kbuf[h], (qmat * (HEAD_DIM ** -0.5)).astype(BF16), _NT,
                              preferred_element_type=F32)
        gate = g_h if gate is None else gate + g_h
        s = s_h if s is None else s + s_h

    blk_i = lax.broadcasted_iota(jnp.int32, (nbp, cols), 0)
    sel_bias = jnp.where(_topk_mask(gate, blk_i, nbp, 0), 0.0, NEG)
    s_past = (s[:past].reshape(nbp, MOBA_BLOCK, cols) + sel_bias[:, None, :]).reshape(past, cols)
    key_i = lax.broadcasted_iota(jnp.int32, (LANES, cols), 0)
    col_i = lax.broadcasted_iota(jnp.int32, (LANES, cols), 1)
    s_own = jnp.where(key_i <= col_i % steps, s[past:], NEG)
    m = jnp.maximum(jnp.max(s_past, axis=0, keepdims=True), jnp.max(s_own, axis=0, keepdims=True))
    p_t = jnp.concatenate([jnp.exp(s_past - m), jnp.exp(s_own - m)], axis=0).T
    l = jnp.sum(p_t, axis=1, keepdims=True)
    p_b = p_t.astype(BF16)
    outs = []
    for h in range(HEADS):
        rows = slice(h * steps, (h + 1) * steps)
        outs.append(jnp.dot(p_b, vbuf[h], preferred_element_type=F32)[rows] / l[rows])
    o_ref[0] = jnp.concatenate(outs, axis=-1)


def _attn_sample(page_table, q, k_new, v_new, cache_k, cache_v, layer):
    bs, steps, w = q.shape
    n_pages = page_table.shape[1]
    page = cache_k.shape[2] // HEADS
    new = pl.BlockSpec((1, steps, w), lambda b, pt: (b, 0, 0))

    def page_spec(p):
        return pl.BlockSpec((1, 1, page * HEADS, HEAD_DIM), lambda b, pt: (layer, pt[b, p], 0, 0))

    keys = n_pages * page + LANES
    return pl.pallas_call(
        functools.partial(_attn_sample_kernel, n_pages=n_pages, page=page, steps=steps),
        grid_spec=pltpu.PrefetchScalarGridSpec(
            num_scalar_prefetch=1,
            grid=(bs,),
            in_specs=[new, new, new] + [page_spec(p) for p in range(n_pages)] * 2,
            out_specs=new,
            scratch_shapes=[pltpu.VMEM((HEADS, keys, HEAD_DIM), BF16),
                            pltpu.VMEM((HEADS, keys, HEAD_DIM), BF16)],
        ),
        out_shape=jax.ShapeDtypeStruct((bs, steps, w), F32),
        compiler_params=_cparams("arbitrary"),
        name="attn_sample",
    )(page_table, q, k_new, v_new, *([cache_k] * n_pages), *([cache_v] * n_pages))


def _merge_kernel(x_ref, ya_ref, yb_ref, yc_ref, gpre_ref, wg_ref, wb_ref, wo_ref, gpost_ref,
                  o_ref, *, d):
    x = x_ref[...]
    xn = (_rms(x) * gpre_ref[...]).astype(BF16)
    merged = None
    for n, y_ref in enumerate((ya_ref, yb_ref, yc_ref)):
        gate = jax.nn.sigmoid(jnp.dot(xn, wg_ref[:, n * d:(n + 1) * d], preferred_element_type=F32))
        proj = jnp.dot(y_ref[...].astype(BF16), wb_ref[n], preferred_element_type=F32)
        merged = gate * proj if merged is None else merged + gate * proj
    mix = jnp.dot(merged.astype(BF16), wo_ref[...], preferred_element_type=F32)
    o_ref[...] = x + _rms(mix) * gpost_ref[...]


def _merge(x, ya, yb, yc, gpre, wg, wb, wo, gpost, *, tm):
    n, d = x.shape
    w = ya.shape[1]
    row = lambda i: (i, 0)
    return pl.pallas_call(
        functools.partial(_merge_kernel, d=d),
        grid=(n // tm,),
        in_specs=[pl.BlockSpec((tm, d), row)] + [pl.BlockSpec((tm, w), row)] * 3
                 + [_const_spec(a.shape) for a in (gpre, wg, wb, wo, gpost)],
        out_specs=pl.BlockSpec((tm, d), row),
        out_shape=jax.ShapeDtypeStruct((n, d), F32),
        compiler_params=_cparams("parallel"),
        name="merge",
    )(x, ya, yb, yc, gpre, wg, wb, wo, gpost)


def _ffn_kernel(x_ref, gpre_ref, wgu_ref, wd_ref, gpost_ref, o_ref, *, d_ff, n_split):
    x = x_ref[...]
    xn = (_rms(x) * gpre_ref[...]).astype(BF16)
    cw = d_ff // n_split
    f = None
    for c in range(n_split):
        g = jnp.dot(xn, wgu_ref[:, c * cw:(c + 1) * cw], preferred_element_type=F32)
        u = jnp.dot(xn, wgu_ref[:, d_ff + c * cw:d_ff + (c + 1) * cw], preferred_element_type=F32)
        act = (jax.nn.silu(g) * u).astype(BF16)
        part = jnp.dot(act, wd_ref[c * cw:(c + 1) * cw, :], preferred_element_type=F32)
        f = part if f is None else f + part
    o_ref[...] = x + _rms(f) * gpost_ref[...]


def _ffn(x, gpre, wgu, wd, gpost, *, tm):
    n, d = x.shape
    d_ff = wd.shape[0]
    n_split = 2 if (d_ff // 2) % LANES == 0 else 1
    row = lambda i: (i, 0)
    return pl.pallas_call(
        functools.partial(_ffn_kernel, d_ff=d_ff, n_split=n_split),
        grid=(n // tm,),
        in_specs=[pl.BlockSpec((tm, d), row)] + [_const_spec(a.shape) for a in (gpre, wgu, wd, gpost)],
        out_specs=pl.BlockSpec((tm, d), row),
        out_shape=jax.ShapeDtypeStruct((n, d), F32),
        compiler_params=_cparams("parallel"),
        name="ffn",
    )(x, gpre, wgu, wd, gpost)


def _row_tile(n):
    for tm in (256, 128, 64, 32, 16, 8):
        if n % tm == 0:
            return tm
    raise ValueError(f"row count {n} is not a multiple of {SUBLANES}")


def _layer_weights(l, P):
    width = P['a_norm_v'].shape[1]
    row = lambda a: a[l].reshape(1, -1)
    w_in = P['w_in'][l].astype(BF16)
    return dict(
        g_mix_pre=row(P['norm_mix_pre']), g_mix_post=row(P['norm_mix_post']),
        g_ffn_pre=row(P['norm_ffn_pre']), g_ffn_post=row(P['norm_ffn_post']),
        w_mix=w_in[:, :7 * width], w_gate=w_in[:, 7 * width:], gv=row(P['a_norm_v']),
        ws=P['a_w_s'][l], bs=P['a_b_s'][l],
        cw=P['c_conv_w'][l], cb=row(P['c_conv_b']),
        wa=P['c_w_a'][l].astype(BF16), ba=row(P['c_b_a']),
        wi=P['c_w_i'][l].astype(BF16), bi=row(P['c_b_i']), lam=row(P['c_lambda']),
        wb=P['w_branch'][l].astype(BF16), wo=P['w_out'][l].astype(BF16),
        wgu=P['ffn_w_gu'][l].astype(BF16), wd=P['ffn_w_down'][l].astype(BF16))


def _finish_layer(x, ya, yb, yc, L, tm):
    x1 = _merge(x, ya, yb, yc, L['g_mix_pre'], L['w_gate'], L['wb'], L['wo'], L['g_mix_post'], tm=tm)
    return _ffn(x1, L['g_ffn_pre'], L['wgu'], L['wd'], L['g_ffn_post'], tm=tm)


def _prompt_layer(x, L, cos, sin):
    b, t, d = x.shape
    width = L['gv'].shape[1]
    tm = _row_tile(t)
    xf = x.reshape(b * t, d)
    u, v, cx, cg, k_rows, v_rows, kb, qt, vt, kmean = _inproj(
        xf, L['g_mix_pre'], L['w_mix'], L['gv'], cos, sin, tm=tm, table_tiles=t // tm, prompt=True)
    seq = lambda a: a.reshape(b, t, width)
    bias_full = jnp.repeat(L['bs'][:, :CHUNK].T, HEAD_DIM, axis=1)
    ya, yc, h_last = _mix_prompt(seq(u), seq(v), seq(cx), seq(cg), L['ws'][:, :CHUNK, :CHUNK], bias_full,
                                 L['cw'], L['cb'], L['wa'], L['ba'], L['wi'], L['bi'], L['lam'],
                                 tt=_row_tile(t))
    yb = _attn_prompt(qt, seq(kb), vt, kmean.reshape(b, t // MOBA_BLOCK, width))
    y = _finish_layer(xf, ya.reshape(b * t, width), yb.reshape(b * t, width),
                      yc.reshape(b * t, width), L, tm)
    heads = lambda a: a.reshape(b, t, HEADS, HEAD_DIM)
    return (y.reshape(b, t, d), heads(k_rows), heads(v_rows), h_last[:, 0],
            seq(cx)[:, t - (CONV_W - 1):], seq(v)[:, t - CHUNK:])


def _sample_layer(l, x, L, cos, sin, page_table, cache_k, cache_v, h0, conv0):
    bs, steps, d = x.shape
    width = L['gv'].shape[1]
    n = bs * steps
    tm = _row_tile(n)
    xf = x.reshape(n, d)
    u, v, cx, cg, q, k, vv = _inproj(xf, L['g_mix_pre'], L['w_mix'], L['gv'], cos, sin,
                                     tm=tm, table_tiles=1, prompt=False)
    w_low = jnp.tril(L['ws'][:, :steps, :steps])
    wrow = jnp.repeat(w_low.transpose(1, 2, 0).reshape(steps * steps, HEADS), HEAD_DIM, axis=1)
    brow = jnp.repeat(L['bs'][:, :steps].T, HEAD_DIM, axis=1)
    ya, yc, h_new = _mix_sample(u, v, cx, cg, wrow, brow, L['cw'], L['cb'], L['wa'], L['ba'],
                                L['wi'], L['bi'], L['lam'], h0, conv0.reshape(bs, -1), steps=steps)
    seq = lambda a: a.reshape(bs, steps, width)
    yb = _attn_sample(page_table, seq(q), seq(k), seq(vv), cache_k, cache_v, l)
    y = _finish_layer(xf, ya, yb.reshape(n, width), yc, L, tm)
    conv_new = jnp.concatenate([conv0, seq(cx)], axis=1)[:, -(CONV_W - 1):]
    heads = lambda a: a.reshape(bs, steps, HEADS, HEAD_DIM)
    return y.reshape(bs, steps, d), heads(k), heads(vv), h_new, conv_new, seq(v)


def kernel(x_prompt, x_sample, cache_k, cache_v, page_table, state_lru_h, state_conv, norm_mix_pre, norm_mix_post, norm_ffn_pre, norm_ffn_post, w_in, a_norm_v, a_w_s, a_b_s, c_conv_w, c_conv_b, c_w_a, c_b_a, c_w_i, c_b_i, c_lambda, w_branch, w_out, ffn_w_gu, ffn_w_down):
    P = dict(norm_mix_pre=norm_mix_pre, norm_mix_post=norm_mix_post, norm_ffn_pre=norm_ffn_pre,
             norm_ffn_post=norm_ffn_post, w_in=w_in, a_norm_v=a_norm_v, a_w_s=a_w_s, a_b_s=a_b_s,
             c_conv_w=c_conv_w, c_conv_b=c_conv_b, c_w_a=c_w_a, c_b_a=c_b_a, c_w_i=c_w_i,
             c_b_i=c_b_i, c_lambda=c_lambda, w_branch=w_branch, w_out=w_out,
             ffn_w_gu=ffn_w_gu, ffn_w_down=ffn_w_down)
    depth = w_in.shape[0]
    tp = x_prompt.shape[1]
    bs, steps, _ = x_sample.shape
    n_pages, page = page_table.shape[1], cache_k.shape[2]
    past = n_pages * page
    assert past % MOBA_BLOCK == 0 and MOBA_BLOCK % page == 0 and tp % MOBA_BLOCK == 0
    assert cache_k.shape[3:] == (HEADS, HEAD_DIM)
    ck = cache_k.reshape(cache_k.shape[0], cache_k.shape[1], page * HEADS, HEAD_DIM)
    cv = cache_v.reshape(ck.shape)

    cos_p, sin_p = _rope_tables(tp, 0, tp)
    cos_s, sin_s = _rope_tables(_row_tile(bs * steps), past, steps)

    yp, ys = x_prompt, x_sample
    outs = [[] for _ in range(10)]
    for l in range(depth):
        L = _layer_weights(l, P)
        yp, k1, v1, h1, c1, a1 = _prompt_layer(yp, L, cos_p, sin_p)
        ys, k2, v2, h2, c2, a2 = _sample_layer(l, ys, L, cos_s, sin_s, page_table, ck, cv,
                                               state_lru_h[l], state_conv[l])
        for lst, val in zip(outs, (k1, v1, k2, v2, h1, h2, c1, c2, a1, a2)):
            lst.append(val)
    return (yp, ys) + tuple(jnp.stack(o) for o in outs)
```

```python
import functools

import jax
import jax.numpy as jnp
from jax import lax
from jax.experimental import pallas as pl
from jax.experimental.pallas import tpu as pltpu

F32 = jnp.float32
BF16 = jnp.bfloat16

EPS = 1e-6
ROPE_THETA = 10000.0
LRU_C = 8.0
HEADS = 4
HEAD_DIM = 128
CHUNK = 128
MOBA_BLOCK = 256
MOBA_TOPK = 3
CONV_W = 4
NEG = -1e30
LOG2_E = 1.4426950408889634
LANES = 128
SUBLANES = 8
VMEM_LIMIT = 56 * 1024 * 1024

_NT = (((1,), (1,)), ((), ()))


def _cparams(*sem):
    return pltpu.CompilerParams(dimension_semantics=sem, vmem_limit_bytes=VMEM_LIMIT)


def _const_spec(shape):
    zeros = (0,) * len(shape)
    return pl.BlockSpec(shape, lambda *_: zeros)


def _rms(x):
    return x * lax.rsqrt(jnp.mean(x * x, axis=-1, keepdims=True) + EPS)


def _rope_table_kernel(inv_ref, cos_ref, sin_ref, *, pos0, period):
    n = cos_ref.shape[0]
    row = lax.broadcasted_iota(jnp.int32, (n, HEAD_DIM), 0)
    lane = lax.broadcasted_iota(jnp.int32, (n, HEAD_DIM), 1)
    pos = (pos0 + row % period).astype(F32)
    ang = pos * inv_ref[...]
    s = jnp.sin(ang)
    cos_ref[...] = jnp.cos(ang)
    sin_ref[...] = jnp.where(lane < HEAD_DIM // 2, -s, s)


def _rope_tables(n_rows, pos0, period):
    half = HEAD_DIM // 2
    inv = ROPE_THETA ** (-jnp.arange(half, dtype=F32) / half)
    inv = jnp.concatenate([inv, inv]).reshape(1, HEAD_DIM)
    return pl.pallas_call(
        functools.partial(_rope_table_kernel, pos0=pos0, period=period),
        out_shape=(jax.ShapeDtypeStruct((n_rows, HEAD_DIM), F32),) * 2,
        name="rope_tables",
    )(inv)


def _rope(z, cos, sin):
    outs = []
    for h in range(HEADS):
        zh = z[:, h * HEAD_DIM:(h + 1) * HEAD_DIM]
        outs.append(zh * cos + pltpu.roll(zh, HEAD_DIM // 2, 1) * sin)
    return jnp.concatenate(outs, axis=-1)


def _lanes(g):
    return slice(g * HEAD_DIM, (g + 1) * HEAD_DIM)


def _store_head_rows(ref, z, tm):
    for h in range(HEADS):
        ref[pl.ds(h, tm, stride=HEADS), :] = z[:, _lanes(h)]


def _inproj_kernel(x_ref, g_ref, w_ref, gv_ref, cos_ref, sin_ref, *outs, width, prompt):
    xn = (_rms(x_ref[...]) * g_ref[...]).astype(BF16)
    tm = x_ref.shape[0]

    def proj(c):
        return jnp.dot(xn, w_ref[:, c * width:(c + 1) * width], preferred_element_type=F32)

    if prompt:
        u_ref, v_ref, cx_ref, cg_ref, ko_ref, vo_ref, kb_ref, qt_ref, vt_ref, km_ref = outs
    else:
        u_ref, v_ref, cx_ref, cg_ref, q_ref, k_ref, vv_ref = outs
    u_ref[...] = jax.nn.gelu(proj(0))
    v_ref[...] = _rms(jax.nn.gelu(proj(1))) * gv_ref[...]
    cx_ref[...] = proj(5)
    cg_ref[...] = jax.nn.gelu(proj(6))
    cos = cos_ref[...]
    sin = sin_ref[...]
    q = _rope(proj(2), cos, sin)
    k = _rope(proj(3), cos, sin)
    vv = proj(4)
    if prompt:
        qt_ref[...] = q.T
        vt_ref[...] = vv.T.astype(BF16)
        kb_ref[...] = k.astype(BF16)
        for j in range(tm // MOBA_BLOCK):
            kj = k[j * MOBA_BLOCK:(j + 1) * MOBA_BLOCK]
            km_ref[j] = jnp.sum(kj, axis=0, keepdims=True) * (1.0 / MOBA_BLOCK)
        _store_head_rows(ko_ref, k, tm)
        _store_head_rows(vo_ref, vv, tm)
    else:
        q_ref[...] = q
        k_ref[...] = k
        vv_ref[...] = vv


def _inproj(x, g, w_mix, gv, cos, sin, *, tm, table_tiles, prompt):
    n, d = x.shape
    width = w_mix.shape[1] // 7
    row = lambda i: (i, 0)
    col = lambda i: (0, i)
    tab = lambda i: (i % table_tiles, 0)
    wide = (jax.ShapeDtypeStruct((n, width), F32), pl.BlockSpec((tm, width), row))
    if prompt:
        heads = (jax.ShapeDtypeStruct((n * HEADS, HEAD_DIM), F32), pl.BlockSpec((tm * HEADS, HEAD_DIM), row))
        nblk = tm // MOBA_BLOCK
        outs = [wide] * 4 + [heads] * 2 + [
            (jax.ShapeDtypeStruct((n, width), BF16), pl.BlockSpec((tm, width), row)),
            (jax.ShapeDtypeStruct((width, n), F32), pl.BlockSpec((width, tm), col)),
            (jax.ShapeDtypeStruct((width, n), BF16), pl.BlockSpec((width, tm), col)),
            (jax.ShapeDtypeStruct((n // MOBA_BLOCK, 1, width), F32),
             pl.BlockSpec((nblk, 1, width), lambda i: (i, 0, 0)))]
    else:
        outs = [wide] * 7
    return pl.pallas_call(
        functools.partial(_inproj_kernel, width=width, prompt=prompt),
        grid=(n // tm,),
        in_specs=[pl.BlockSpec((tm, d), row), _const_spec((1, d)), _const_spec(w_mix.shape),
                  _const_spec((1, width)), pl.BlockSpec((tm, HEAD_DIM), tab),
                  pl.BlockSpec((tm, HEAD_DIM), tab)],
        out_specs=[o[1] for o in outs],
        out_shape=tuple(o[0] for o in outs),
        compiler_params=_cparams("parallel"),
        name="inproj_prompt" if prompt else "inproj_sample",
    )(x, g, w_mix, gv, cos, sin)


def _softplus(x):
    return jnp.maximum(x, 0.0) + jnp.log1p(jnp.exp(-jnp.abs(x)))


def _lru_coeffs(xc, g, wa_ref, ba_ref, wi_ref, bi_ref, lam_ref):
    xb = xc.astype(BF16)
    r = jax.nn.sigmoid(jnp.dot(xb, wa_ref[g], preferred_element_type=F32) + ba_ref[:, _lanes(g)])
    gate_i = jax.nn.sigmoid(jnp.dot(xb, wi_ref[g], preferred_element_type=F32) + bi_ref[:, _lanes(g)])
    log_a = (-LRU_C) * r * _softplus(-lam_ref[:, _lanes(g)])
    a = jnp.exp(log_a)
    th = jnp.tanh(log_a)
    mult = jnp.sqrt(-2.0 * th / (1.0 - th))
    return a, xc * gate_i * mult


SCAN_ROWS = SUBLANES * SUBLANES


def _mix_prompt_kernel(u_ref, v_ref, cx_ref, cg_ref, ws_ref, bias_ref, cw_ref, cb_ref,
                       wa_ref, ba_ref, wi_ref, bi_ref, lam_ref,
                       ya_ref, yc_ref, h_ref,
                       xbuf, a_scr, x_scr, hs_scr, h_scr, *, tt):
    t_idx = pl.program_id(1)

    @pl.when(t_idx == 0)
    def _():
        xbuf[0:SUBLANES, :] = jnp.zeros((SUBLANES, xbuf.shape[1]), F32)
        h_scr[...] = jnp.zeros(h_scr.shape, F32)

    r_i = lax.broadcasted_iota(jnp.int32, (CHUNK, CHUNK), 0)
    c_i = lax.broadcasted_iota(jnp.int32, (CHUNK, CHUNK), 1)
    tril = c_i <= r_i
    w_low = [jnp.where(tril, ws_ref[g], 0.0).astype(BF16) for g in range(HEADS)]
    for c in range(tt // CHUNK):
        rows = slice(c * CHUNK, (c + 1) * CHUNK)
        vc = v_ref[0, rows, :].astype(BF16)
        mixed = jnp.concatenate(
            [jnp.dot(w_low[g], vc[:, g * HEAD_DIM:(g + 1) * HEAD_DIM], preferred_element_type=F32)
             for g in range(HEADS)], axis=-1)
        ya_ref[0, rows, :] = u_ref[0, rows, :] * (mixed + bias_ref[...])

    xbuf[SUBLANES:SUBLANES + tt, :] = cx_ref[0]
    xc = cb_ref[...] + cw_ref[CONV_W - 1:CONV_W, :] * xbuf[SUBLANES:SUBLANES + tt, :]
    for d in range(1, CONV_W):
        xc = xc + cw_ref[CONV_W - 1 - d:CONV_W - d, :] * xbuf[SUBLANES - d:SUBLANES - d + tt, :]
    xbuf[0:SUBLANES, :] = xbuf[tt:tt + SUBLANES, :]

    for g in range(HEADS):
        a, xin = _lru_coeffs(xc[:, _lanes(g)], g, wa_ref, ba_ref, wi_ref, bi_ref, lam_ref)
        a_scr[g] = a
        x_scr[g] = xin
        h = h_scr[0:1, _lanes(g)]
        for s in range(tt // SCAN_ROWS):
            base = s * SCAN_ROWS
            p_run, s_run = [], []
            for j in range(SUBLANES):
                a_j = a_scr[g, pl.ds(base + j, SUBLANES, stride=SUBLANES), :]
                x_j = x_scr[g, pl.ds(base + j, SUBLANES, stride=SUBLANES), :]
                if j == 0:
                    p_run.append(a_j)
                    s_run.append(x_j)
                else:
                    p_run.append(p_run[-1] * a_j)
                    s_run.append(a_j * s_run[-1] + x_j)
            carry_in = []
            for m in range(SUBLANES):
                carry_in.append(h)
                h = s_run[-1][m:m + 1, :] + p_run[-1][m:m + 1, :] * h
            cin = jnp.concatenate(carry_in, axis=0)
            for j in range(SUBLANES):
                hs_scr[g, pl.ds(base + j, SUBLANES, stride=SUBLANES), :] = s_run[j] + p_run[j] * cin
        h_scr[:, _lanes(g)] = jnp.broadcast_to(h, (SUBLANES, HEAD_DIM))
    h_ref[0] = h_scr[...]
    yc_ref[0] = jnp.concatenate([hs_scr[g] for g in range(HEADS)], axis=-1) * cg_ref[0]


def _mix_prompt(u, v, cx, cg, ws, bias_full, cw, cb, wa, ba, wi, bi, lam, *, tt):
    b, t, w = u.shape
    seq = pl.BlockSpec((1, tt, w), lambda i, j: (i, j, 0))
    ins = [u, v, cx, cg, ws, bias_full, cw, cb, wa, ba, wi, bi, lam]
    slab = pltpu.VMEM((HEADS, tt, HEAD_DIM), F32)
    return pl.pallas_call(
        functools.partial(_mix_prompt_kernel, tt=tt),
        grid=(b, t // tt),
        in_specs=[seq] * 4 + [_const_spec(a.shape) for a in ins[4:]],
        out_specs=[seq, seq, pl.BlockSpec((1, SUBLANES, w), lambda i, j: (i, 0, 0))],
        out_shape=(jax.ShapeDtypeStruct((b, t, w), F32), jax.ShapeDtypeStruct((b, t, w), F32),
                   jax.ShapeDtypeStruct((b, SUBLANES, w), F32)),
        scratch_shapes=[pltpu.VMEM((tt + SUBLANES, w), F32), slab, slab, slab,
                        pltpu.VMEM((SUBLANES, w), F32)],
        compiler_params=_cparams("arbitrary", "arbitrary"),
        name="mix_prompt",
    )(*ins)


def _mix_sample_kernel(u_ref, v_ref, cx_ref, cg_ref, wrow_ref, brow_ref, cw_ref, cb_ref,
                       wa_ref, ba_ref, wi_ref, bi_ref, lam_ref, h0_ref, conv0_ref,
                       ya_ref, yc_ref, h_ref, in_scr, out_scr, *, steps, nseq):
    w = u_ref.shape[1]
    for n, ref in enumerate((u_ref, v_ref, cx_ref, cg_ref)):
        for g in range(HEADS):
            in_scr[n * HEADS + g] = ref[:, _lanes(g)]

    def at_step(n, g, t):
        return in_scr[n * HEADS + g, pl.ds(t, nseq, stride=steps), :]

    for g in range(HEADS):
        v_t = [at_step(1, g, t) for t in range(steps)]
        for t in range(steps):
            mixed = brow_ref[t:t + 1, _lanes(g)]
            for s in range(t + 1):
                mixed = mixed + wrow_ref[t * steps + s:t * steps + s + 1, _lanes(g)] * v_t[s]
            out_scr[g, pl.ds(t, nseq, stride=steps), :] = at_step(0, g, t) * mixed

        xp = [conv0_ref[:, j * w + g * HEAD_DIM:j * w + (g + 1) * HEAD_DIM] for j in range(CONV_W - 1)]
        xp += [at_step(2, g, t) for t in range(steps)]
        h = h0_ref[:, _lanes(g)]
        for t in range(steps):
            xc = cb_ref[:, _lanes(g)]
            for j in range(CONV_W):
                xc = xc + cw_ref[j:j + 1, _lanes(g)] * xp[t + j]
            a, xin = _lru_coeffs(xc, g, wa_ref, ba_ref, wi_ref, bi_ref, lam_ref)
            h = a * h + xin
            out_scr[HEADS + g, pl.ds(t, nseq, stride=steps), :] = h * at_step(3, g, t)
        h_ref[:, _lanes(g)] = h
    ya_ref[...] = jnp.concatenate([out_scr[g] for g in range(HEADS)], axis=-1)
    yc_ref[...] = jnp.concatenate([out_scr[HEADS + g] for g in range(HEADS)], axis=-1)


def _mix_sample(u, v, cx, cg, wrow, brow, cw, cb, wa, ba, wi, bi, lam, h0, conv0, *, steps):
    n, w = u.shape
    nseq = n // steps
    ins = [u, v, cx, cg, wrow, brow, cw, cb, wa, ba, wi, bi, lam, h0, conv0]
    return pl.pallas_call(
        functools.partial(_mix_sample_kernel, steps=steps, nseq=nseq),
        grid=(1,),
        in_specs=[_const_spec(a.shape) for a in ins],
        out_specs=[_const_spec((n, w)), _const_spec((n, w)), _const_spec((nseq, w))],
        out_shape=(jax.ShapeDtypeStruct((n, w), F32), jax.ShapeDtypeStruct((n, w), F32),
                   jax.ShapeDtypeStruct((nseq, w), F32)),
        scratch_shapes=[pltpu.VMEM((4 * HEADS, n, HEAD_DIM), F32),
                        pltpu.VMEM((2 * HEADS, n, HEAD_DIM), F32)],
        compiler_params=_cparams("arbitrary"),
        name="mix_sample",
    )(*ins)


def _topk_mask(g, idx, n, axis):
    sel = jnp.zeros(g.shape, jnp.bool_)
    for _ in range(MOBA_TOPK):
        best = jnp.max(g, axis=axis, keepdims=True)
        first = jnp.min(jnp.where(g == best, idx, n), axis=axis, keepdims=True)
        pick = idx == first
        sel = jnp.logical_or(sel, pick)
        g = jnp.where(pick, -jnp.inf, g)
    return sel


SUM_ROWS = 16


def _attn_prompt_kernel(qt_ref, kb_ref, vt_ref, km_ref, o_ref, bias_scr, *, nb, group):
    blk = MOBA_BLOCK
    span = group * blk
    i = pl.program_id(1)
    own = pl.multiple_of(i * blk, blk)
    blk_i = lax.broadcasted_iota(jnp.int32, (nb, blk), 0)
    past = blk_i < i
    key_i = lax.broadcasted_iota(jnp.int32, (blk, blk), 0)
    qry_i = lax.broadcasted_iota(jnp.int32, (blk, blk), 1)

    def values(h, start, size):
        ones = jnp.ones((SUM_ROWS, size), BF16)
        return jnp.concatenate([vt_ref[_lanes(h), pl.ds(start, size)], ones], axis=0)

    qs = []
    for h in range(HEADS):
        qt = qt_ref[_lanes(h), :]
        gate = jnp.dot(km_ref[0, :, _lanes(h)], qt, precision=lax.Precision.HIGHEST,
                       preferred_element_type=F32)
        sel = jnp.logical_and(_topk_mask(jnp.where(past, gate, -jnp.inf), blk_i, nb, 0), past)
        sel_bias = jnp.where(sel, 0.0, NEG)
        for sj in range(nb // group):
            bias_scr[h, sj, 0:group, :] = sel_bias[sj * group:(sj + 1) * group, :]
        qs.append((qt * (HEAD_DIM ** -0.5 * LOG2_E)).astype(BF16))

    def logits(h, start, size):
        return jnp.dot(kb_ref[0, pl.ds(start, size), _lanes(h)], qs[h], preferred_element_type=F32)

    m0, acc0 = [], []
    s_next = logits(0, own, blk)
    for h in range(HEADS):
        s = jnp.where(key_i <= qry_i, s_next, NEG)
        if h + 1 < HEADS:
            s_next = logits(h + 1, own, blk)
        m0.append(jnp.max(s, axis=0, keepdims=True))
        acc0.append(jnp.dot(values(h, own, blk), jnp.exp2(s - m0[h]).astype(BF16),
                            preferred_element_type=F32))

    def body(sj, carry):
        start = pl.multiple_of(sj * span, span)

        out = []
        s_next = logits(0, start, span)
        for h in range(HEADS):
            m, acc = carry[h]
            s = s_next
            if h + 1 < HEADS:
                s_next = logits(h + 1, start, span)
            bias = bias_scr[h, sj, 0:group, :]
            s = (s.reshape(group, blk, blk) + bias[:, None, :]).reshape(span, blk)
            m_new = jnp.maximum(m, jnp.max(s, axis=0, keepdims=True))
            p = jnp.exp2(s - m_new).astype(BF16)
            acc = jnp.exp2(m - m_new) * acc + jnp.dot(values(h, start, span), p,
                                                      preferred_element_type=F32)
            out.append((m_new, acc))
        return tuple(out)

    fin = lax.fori_loop(0, (i + group - 1) // group, body, tuple(zip(m0, acc0)))
    o_ref[0] = jnp.concatenate(
        [(acc[:HEAD_DIM] / acc[HEAD_DIM:HEAD_DIM + 1]).T for _, acc in fin], axis=-1)


def _attn_prompt(qt, kb, vt, kmean):
    b, t, w = kb.shape
    nb = t // MOBA_BLOCK
    group = 4 if nb % 4 == 0 else 1
    return pl.pallas_call(
        functools.partial(_attn_prompt_kernel, nb=nb, group=group),
        grid=(b, nb),
        in_specs=[pl.BlockSpec((w, MOBA_BLOCK), lambda bi, i: (0, bi * nb + i)),
                  pl.BlockSpec((1, t, w), lambda bi, i: (bi, 0, 0)),
                  pl.BlockSpec((w, t), lambda bi, i: (0, bi)),
                  pl.BlockSpec((1, nb, w), lambda bi, i: (bi, 0, 0))],
        out_specs=pl.BlockSpec((1, MOBA_BLOCK, w), lambda bi, i: (bi, i, 0)),
        out_shape=jax.ShapeDtypeStruct((b, t, w), F32),
        scratch_shapes=[pltpu.VMEM((HEADS, nb // group, SUBLANES, MOBA_BLOCK), F32)],
        compiler_params=_cparams("arbitrary", "arbitrary"),
        name="attn_prompt",
    )(qt, kb, vt, kmean)


def _attn_sample_kernel(pt_ref, q_ref, kn_ref, vn_ref, *rest, n_pages, page, steps):
    del pt_ref
    kp = rest[:n_pages]
    vp = rest[n_pages:2 * n_pages]
    o_ref, kbuf, vbuf = rest[2 * n_pages:]
    past = n_pages * page
    nbp = past // MOBA_BLOCK
    per_blk = MOBA_BLOCK // page
    cols = LANES

    w = HEADS * HEAD_DIM
    sums = []
    for p in range(n_pages):
        kpg = jnp.concatenate([kp[p][0, 0, pl.ds(h, page, stride=HEADS), :] for h in range(HEADS)], axis=-1)
        vpg = jnp.concatenate([vp[p][0, 0, pl.ds(h, page, stride=HEADS), :] for h in range(HEADS)], axis=-1)
        sums.append(jnp.sum(kpg, axis=0, keepdims=True))
        kbuf[p * page:(p + 1) * page, :] = kpg.astype(BF16)
        vbuf[p * page:(p + 1) * page, :] = vpg.astype(BF16)
    pad = jnp.zeros((LANES - steps, w), F32)
    kbuf[past:past + LANES, :] = jnp.concatenate([kn_ref[0], pad], axis=0).astype(BF16)
    vbuf[past:past + LANES, :] = jnp.concatenate([vn_ref[0], pad], axis=0).astype(BF16)
    kmean = jnp.concatenate(
        [sum(sums[j * per_blk:(j + 1) * per_blk]) for j in range(nbp)], axis=0) * (1.0 / MOBA_BLOCK)

    q = q_ref[0]
    r_i = lax.broadcasted_iota(jnp.int32, (cols, w), 0)
    l_i = lax.broadcasted_iota(jnp.int32, (cols, w), 1)
    qmat = jnp.where(r_i // steps == l_i // HEAD_DIM, jnp.concatenate([q] * (cols // steps), axis=0), 0.0)
    gate = lax.dot_general(kmean, qmat, _NT, precision=lax.Precision.HIGHEST,
                           preferred_element_type=F32)
    s = lax.dot_general(kbuf[...], (qmat * (HEAD_DIM ** -0.5)).astype(BF16), _NT,
                        preferred_element_type=F32)

    blk_i = lax.broadcasted_iota(jnp.int32, (nbp, cols), 0)
    sel_bias = jnp.where(_topk_mask(gate, blk_i, nbp, 0), 0.0, NEG)
    s_past = (s[:past].reshape(nbp, MOBA_BLOCK, cols) + sel_bias[:, None, :]).reshape(past, cols)
    key_i = lax.broadcasted_iota(jnp.int32, (LANES, cols), 0)
    col_i = lax.broadcasted_iota(jnp.int32, (LANES, cols), 1)
    s_own = jnp.where(key_i <= col_i % steps, s[past:], NEG)
    m = jnp.maximum(jnp.max(s_past, axis=0, keepdims=True), jnp.max(s_own, axis=0, keepdims=True))
    p_t = jnp.concatenate([jnp.exp(s_past - m), jnp.exp(s_own - m)], axis=0).T
    l = jnp.sum(p_t, axis=1, keepdims=True)
    out = jnp.dot(p_t.astype(BF16), vbuf[...], preferred_element_type=F32) / l
    o_ref[0] = jnp.concatenate(
        [out[h * steps:(h + 1) * steps, _lanes(h)] for h in range(HEADS)], axis=-1)


def _attn_sample(page_table, q, k_new, v_new, cache_k, cache_v, layer):
    bs, steps, w = q.shape
    n_pages = page_table.shape[1]
    page = cache_k.shape[2] // HEADS
    new = pl.BlockSpec((1, steps, w), lambda b, pt: (b, 0, 0))

    def page_spec(p):
        return pl.BlockSpec((1, 1, page * HEADS, HEAD_DIM), lambda b, pt: (layer, pt[b, p], 0, 0))

    keys = n_pages * page + LANES
    return pl.pallas_call(
        functools.partial(_attn_sample_kernel, n_pages=n_pages, page=page, steps=steps),
        grid_spec=pltpu.PrefetchScalarGridSpec(
            num_scalar_prefetch=1,
            grid=(bs,),
            in_specs=[new, new, new] + [page_spec(p) for p in range(n_pages)] * 2,
            out_specs=new,
            scratch_shapes=[pltpu.VMEM((keys, w), BF16), pltpu.VMEM((keys, w), BF16)],
        ),
        out_shape=jax.ShapeDtypeStruct((bs, steps, w), F32),
        compiler_params=_cparams("arbitrary"),
        name="attn_sample",
    )(page_table, q, k_new, v_new, *([cache_k] * n_pages), *([cache_v] * n_pages))


def _merge_kernel(x_ref, ya_ref, yb_ref, yc_ref, gpre_ref, wg_ref, wb_ref, wo_ref, gpost_ref,
                  o_ref, *, d):
    x = x_ref[...]
    xn = (_rms(x) * gpre_ref[...]).astype(BF16)
    merged = None
    for n, y_ref in enumerate((ya_ref, yb_ref, yc_ref)):
        gate = jax.nn.sigmoid(jnp.dot(xn, wg_ref[:, n * d:(n + 1) * d], preferred_element_type=F32))
        proj = jnp.dot(y_ref[...].astype(BF16), wb_ref[n], preferred_element_type=F32)
        merged = gate * proj if merged is None else merged + gate * proj
    mix = jnp.dot(merged.astype(BF16), wo_ref[...], preferred_element_type=F32)
    o_ref[...] = x + _rms(mix) * gpost_ref[...]


def _merge(x, ya, yb, yc, gpre, wg, wb, wo, gpost, *, tm):
    n, d = x.shape
    w = ya.shape[1]
    row = lambda i: (i, 0)
    return pl.pallas_call(
        functools.partial(_merge_kernel, d=d),
        grid=(n // tm,),
        in_specs=[pl.BlockSpec((tm, d), row)] + [pl.BlockSpec((tm, w), row)] * 3
                 + [_const_spec(a.shape) for a in (gpre, wg, wb, wo, gpost)],
        out_specs=pl.BlockSpec((tm, d), row),
        out_shape=jax.ShapeDtypeStruct((n, d), F32),
        compiler_params=_cparams("parallel"),
        name="merge",
    )(x, ya, yb, yc, gpre, wg, wb, wo, gpost)


def _ffn_kernel(x_ref, gpre_ref, wgu_ref, wd_ref, gpost_ref, o_ref, *, d_ff, n_split):
    x = x_ref[...]
    xn = (_rms(x) * gpre_ref[...]).astype(BF16)
    cw = d_ff // n_split
    f = None
    for c in range(n_split):
        g = jnp.dot(xn, wgu_ref[:, c * cw:(c + 1) * cw], preferred_element_type=F32)
        u = jnp.dot(xn, wgu_ref[:, d_ff + c * cw:d_ff + (c + 1) * cw], preferred_element_type=F32)
        act = (jax.nn.silu(g) * u).astype(BF16)
        part = jnp.dot(act, wd_ref[c * cw:(c + 1) * cw, :], preferred_element_type=F32)
        f = part if f is None else f + part
    o_ref[...] = x + _rms(f) * gpost_ref[...]


def _ffn(x, gpre, wgu, wd, gpost, *, tm):
    n, d = x.shape
    d_ff = wd.shape[0]
    n_split = 2 if (d_ff // 2) % LANES == 0 else 1
    row = lambda i: (i, 0)
    return pl.pallas_call(
        functools.partial(_ffn_kernel, d_ff=d_ff, n_split=n_split),
        grid=(n // tm,),
        in_specs=[pl.BlockSpec((tm, d), row)] + [_const_spec(a.shape) for a in (gpre, wgu, wd, gpost)],
        out_specs=pl.BlockSpec((tm, d), row),
        out_shape=jax.ShapeDtypeStruct((n, d), F32),
        compiler_params=_cparams("parallel"),
        name="ffn",
    )(x, gpre, wgu, wd, gpost)


def _row_tile(n):
    for tm in (256, 128, 64, 32, 16, 8):
        if n % tm == 0:
            return tm
    raise ValueError(f"row count {n} is not a multiple of {SUBLANES}")


def _layer_weights(l, P):
    width = P['a_norm_v'].shape[1]
    row = lambda a: a[l].reshape(1, -1)
    w_in = P['w_in'][l].astype(BF16)
    return dict(
        g_mix_pre=row(P['norm_mix_pre']), g_mix_post=row(P['norm_mix_post']),
        g_ffn_pre=row(P['norm_ffn_pre']), g_ffn_post=row(P['norm_ffn_post']),
        w_mix=w_in[:, :7 * width], w_gate=w_in[:, 7 * width:], gv=row(P['a_norm_v']),
        ws=P['a_w_s'][l], bs=P['a_b_s'][l],
        cw=P['c_conv_w'][l], cb=row(P['c_conv_b']),
        wa=P['c_w_a'][l].astype(BF16), ba=row(P['c_b_a']),
        wi=P['c_w_i'][l].astype(BF16), bi=row(P['c_b_i']), lam=row(P['c_lambda']),
        wb=P['w_branch'][l].astype(BF16), wo=P['w_out'][l].astype(BF16),
        wgu=P['ffn_w_gu'][l].astype(BF16), wd=P['ffn_w_down'][l].astype(BF16))


def _finish_layer(x, ya, yb, yc, L, tm):
    x1 = _merge(x, ya, yb, yc, L['g_mix_pre'], L['w_gate'], L['wb'], L['wo'], L['g_mix_post'], tm=tm)
    return _ffn(x1, L['g_ffn_pre'], L['wgu'], L['wd'], L['g_ffn_post'], tm=tm)


def _prompt_layer(x, L, cos, sin):
    b, t, d = x.shape
    width = L['gv'].shape[1]
    tm = _row_tile(t)
    xf = x.reshape(b * t, d)
    u, v, cx, cg, k_rows, v_rows, kb, qt, vt, kmean = _inproj(
        xf, L['g_mix_pre'], L['w_mix'], L['gv'], cos, sin, tm=tm, table_tiles=t // tm, prompt=True)
    seq = lambda a: a.reshape(b, t, width)
    bias_full = jnp.repeat(L['bs'][:, :CHUNK].T, HEAD_DIM, axis=1)
    ya, yc, h_last = _mix_prompt(seq(u), seq(v), seq(cx), seq(cg), L['ws'][:, :CHUNK, :CHUNK], bias_full,
                                 L['cw'], L['cb'], L['wa'], L['ba'], L['wi'], L['bi'], L['lam'],
                                 tt=_row_tile(t))
    yb = _attn_prompt(qt, seq(kb), vt, kmean.reshape(b, t // MOBA_BLOCK, width))
    y = _finish_layer(xf, ya.reshape(b * t, width), yb.reshape(b * t, width),
                      yc.reshape(b * t, width), L, tm)
    heads = lambda a: a.reshape(b, t, HEADS, HEAD_DIM)
    return (y.reshape(b, t, d), heads(k_rows), heads(v_rows), h_last[:, 0],
            seq(cx)[:, t - (CONV_W - 1):], seq(v)[:, t - CHUNK:])


def _sample_layer(l, x, L, cos, sin, page_table, cache_k, cache_v, h0, conv0):
    bs, steps, d = x.shape
    width = L['gv'].shape[1]
    n = bs * steps
    tm = _row_tile(n)
    xf = x.reshape(n, d)
    u, v, cx, cg, q, k, vv = _inproj(xf, L['g_mix_pre'], L['w_mix'], L['gv'], cos, sin,
                                     tm=tm, table_tiles=1, prompt=False)
    w_low = jnp.tril(L['ws'][:, :steps, :steps])
    wrow = jnp.repeat(w_low.transpose(1, 2, 0).reshape(steps * steps, HEADS), HEAD_DIM, axis=1)
    brow = jnp.repeat(L['bs'][:, :steps].T, HEAD_DIM, axis=1)
    ya, yc, h_new = _mix_sample(u, v, cx, cg, wrow, brow, L['cw'], L['cb'], L['wa'], L['ba'],
                                L['wi'], L['bi'], L['lam'], h0, conv0.reshape(bs, -1), steps=steps)
    seq = lambda a: a.reshape(bs, steps, width)
    yb = _attn_sample(page_table, seq(q), seq(k), seq(vv), cache_k, cache_v, l)
    y = _finish_layer(xf, ya, yb.reshape(n, width), yc, L, tm)
    conv_new = jnp.concatenate([conv0, seq(cx)], axis=1)[:, -(CONV_W - 1):]
    heads = lambda a: a.reshape(bs, steps, HEADS, HEAD_DIM)
    return y.reshape(bs, steps, d), heads(k), heads(vv), h_new, conv_new, seq(v)


def kernel(x_prompt, x_sample, cache_k, cache_v, page_table, state_lru_h, state_conv, norm_mix_pre, norm_mix_post, norm_ffn_pre, norm_ffn_post, w_in, a_norm_v, a_w_s, a_b_s, c_conv_w, c_conv_b, c_w_a, c_b_a, c_w_i, c_b_i, c_lambda, w_branch, w_out, ffn_w_gu, ffn_w_down):
    P = dict(norm_mix_pre=norm_mix_pre, norm_mix_post=norm_mix_post, norm_ffn_pre=norm_ffn_pre,
             norm_ffn_post=norm_ffn_post, w_in=w_in, a_norm_v=a_norm_v, a_w_s=a_w_s, a_b_s=a_b_s,
             c_conv_w=c_conv_w, c_conv_b=c_conv_b, c_w_a=c_w_a, c_b_a=c_b_a, c_w_i=c_w_i,
             c_b_i=c_b_i, c_lambda=c_lambda, w_branch=w_branch, w_out=w_out,
             ffn_w_gu=ffn_w_gu, ffn_w_down=ffn_w_down)
    depth = w_in.shape[0]
    tp = x_prompt.shape[1]
    bs, steps, _ = x_sample.shape
    n_pages, page = page_table.shape[1], cache_k.shape[2]
    past = n_pages * page
    assert past % MOBA_BLOCK == 0 and MOBA_BLOCK % page == 0 and tp % MOBA_BLOCK == 0
    assert cache_k.shape[3:] == (HEADS, HEAD_DIM)
    ck = cache_k.reshape(cache_k.shape[0], cache_k.shape[1], page * HEADS, HEAD_DIM)
    cv = cache_v.reshape(ck.shape)

    cos_p, sin_p = _rope_tables(tp, 0, tp)
    cos_s, sin_s = _rope_tables(_row_tile(bs * steps), past, steps)

    yp, ys = x_prompt, x_sample
    outs = [[] for _ in range(10)]
    for l in range(depth):
        L = _layer_weights(l, P)
        yp, k1, v1, h1, c1, a1 = _prompt_layer(yp, L, cos_p, sin_p)
        ys, k2, v2, h2, c2, a2 = _sample_layer(l, ys, L, cos_s, sin_s, page_table, ck, cv,
                                               state_lru_h[l], state_conv[l])
        for lst, val in zip(outs, (k1, v1, k2, v2, h1, h2, c1, c2, a1, a2)):
            lst.append(val)
    return (yp, ys) + tuple(jnp.stack(o) for o in outs)
```

```python
import functools

import jax
import jax.numpy as jnp
from jax import lax
from jax.experimental import pallas as pl
from jax.experimental.pallas import tpu as pltpu

F32 = jnp.float32
BF16 = jnp.bfloat16

EPS = 1e-6
ROPE_THETA = 10000.0
LRU_C = 8.0
HEADS = 4
HEAD_DIM = 128
CHUNK = 128
MOBA_BLOCK = 256
MOBA_TOPK = 3
CONV_W = 4
NEG = -1e30
LOG2_E = 1.4426950408889634
LANES = 128
SUBLANES = 8
VMEM_LIMIT = 56 * 1024 * 1024

_NT = (((1,), (1,)), ((), ()))


def _cparams(*sem):
    return pltpu.CompilerParams(dimension_semantics=sem, vmem_limit_bytes=VMEM_LIMIT)


def _const_spec(shape):
    zeros = (0,) * len(shape)
    return pl.BlockSpec(shape, lambda *_: zeros, pipeline_mode=pl.Buffered(1))


def _layer_spec(stacked_shape, l):
    zeros = (0,) * (len(stacked_shape) - 1)
    return pl.BlockSpec((None,) + tuple(stacked_shape[1:]), lambda *_: (l,) + zeros,
                        pipeline_mode=pl.Buffered(1))


def _rms(x):
    return x * lax.rsqrt(jnp.mean(x * x, axis=-1, keepdims=True) + EPS)


def _rope_table_kernel(inv_ref, cos_ref, sin_ref, *, pos0, period):
    n = cos_ref.shape[0]
    row = lax.broadcasted_iota(jnp.int32, (n, HEAD_DIM), 0)
    lane = lax.broadcasted_iota(jnp.int32, (n, HEAD_DIM), 1)
    pos = (pos0 + row % period).astype(F32)
    ang = pos * inv_ref[...]
    s = jnp.sin(ang)
    cos_ref[...] = jnp.cos(ang)
    sin_ref[...] = jnp.where(lane < HEAD_DIM // 2, -s, s)


def _rope_tables(n_rows, pos0, period):
    half = HEAD_DIM // 2
    inv = ROPE_THETA ** (-jnp.arange(half, dtype=F32) / half)
    inv = jnp.concatenate([inv, inv]).reshape(1, HEAD_DIM)
    return pl.pallas_call(
        functools.partial(_rope_table_kernel, pos0=pos0, period=period),
        out_shape=(jax.ShapeDtypeStruct((n_rows, HEAD_DIM), F32),) * 2,
        name="rope_tables",
    )(inv)


def _rope(z, cos, sin):
    outs = []
    for h in range(HEADS):
        zh = z[:, h * HEAD_DIM:(h + 1) * HEAD_DIM]
        outs.append(zh * cos + pltpu.roll(zh, HEAD_DIM // 2, 1) * sin)
    return jnp.concatenate(outs, axis=-1)


def _lanes(g):
    return slice(g * HEAD_DIM, (g + 1) * HEAD_DIM)


def _store_head_rows(ref, z, tm):
    for h in range(HEADS):
        ref[pl.ds(h, tm, stride=HEADS), :] = z[:, _lanes(h)]


def _inproj_kernel(x_ref, g_ref, w_ref, gv_ref, cos_ref, sin_ref, *outs, width, prompt):
    xn = (_rms(x_ref[...]) * g_ref[...]).astype(BF16)
    tm = x_ref.shape[0]

    def proj(c):
        return jnp.dot(xn, w_ref[:, c * width:(c + 1) * width], preferred_element_type=F32)

    if prompt:
        u_ref, v_ref, cx_ref, cg_ref, ko_ref, vo_ref, kb_ref, qt_ref, vt_ref, km_ref = outs
    else:
        u_ref, v_ref, cx_ref, cg_ref, q_ref, k_ref, vv_ref = outs
    u_ref[...] = jax.nn.gelu(proj(0))
    v_ref[...] = _rms(jax.nn.gelu(proj(1))) * gv_ref[...]
    cx_ref[...] = proj(5)
    cg_ref[...] = jax.nn.gelu(proj(6))
    cos = cos_ref[...]
    sin = sin_ref[...]
    q = _rope(proj(2), cos, sin)
    k = _rope(proj(3), cos, sin)
    vv = proj(4)
    if prompt:
        qt_ref[...] = q.T
        vt_ref[...] = vv.T.astype(BF16)
        kb_ref[...] = k.astype(BF16)
        for j in range(tm // MOBA_BLOCK):
            kj = k[j * MOBA_BLOCK:(j + 1) * MOBA_BLOCK]
            km_ref[j] = jnp.sum(kj, axis=0, keepdims=True) * (1.0 / MOBA_BLOCK)
        _store_head_rows(ko_ref, k, tm)
        _store_head_rows(vo_ref, vv, tm)
    else:
        q_ref[...] = q
        k_ref[...] = k
        vv_ref[...] = vv


def _inproj(x, g, w_in, l, gv, cos, sin, *, tm, table_tiles, prompt):
    n, d = x.shape
    width = gv.shape[1]
    row = lambda i: (i, 0)
    col = lambda i: (0, i)
    tab = lambda i: (i % table_tiles, 0)
    wide = (jax.ShapeDtypeStruct((n, width), F32), pl.BlockSpec((tm, width), row))
    if prompt:
        heads = (jax.ShapeDtypeStruct((n * HEADS, HEAD_DIM), F32), pl.BlockSpec((tm * HEADS, HEAD_DIM), row))
        nblk = tm // MOBA_BLOCK
        outs = [wide] * 4 + [heads] * 2 + [
            (jax.ShapeDtypeStruct((n, width), BF16), pl.BlockSpec((tm, width), row)),
            (jax.ShapeDtypeStruct((width, n), F32), pl.BlockSpec((width, tm), col)),
            (jax.ShapeDtypeStruct((width, n), BF16), pl.BlockSpec((width, tm), col)),
            (jax.ShapeDtypeStruct((n // MOBA_BLOCK, 1, width), F32),
             pl.BlockSpec((nblk, 1, width), lambda i: (i, 0, 0)))]
    else:
        outs = [wide] * 7
    return pl.pallas_call(
        functools.partial(_inproj_kernel, width=width, prompt=prompt),
        grid=(n // tm,),
        in_specs=[pl.BlockSpec((tm, d), row), _const_spec((1, d)),
                  _layer_spec((w_in.shape[0], d, 7 * width), l),
                  _const_spec((1, width)), pl.BlockSpec((tm, HEAD_DIM), tab),
                  pl.BlockSpec((tm, HEAD_DIM), tab)],
        out_specs=[o[1] for o in outs],
        out_shape=tuple(o[0] for o in outs),
        compiler_params=_cparams("parallel"),
        name="inproj_prompt" if prompt else "inproj_sample",
    )(x, g, w_in, gv, cos, sin)


def _softplus(x):
    return jnp.maximum(x, 0.0) + jnp.log1p(jnp.exp(-jnp.abs(x)))


def _lru_coeffs(xc, g, wa_ref, ba_ref, wi_ref, bi_ref, lam_ref):
    xb = xc.astype(BF16)
    r = jax.nn.sigmoid(jnp.dot(xb, wa_ref[g], preferred_element_type=F32) + ba_ref[:, _lanes(g)])
    gate_i = jax.nn.sigmoid(jnp.dot(xb, wi_ref[g], preferred_element_type=F32) + bi_ref[:, _lanes(g)])
    log_a = (-LRU_C) * r * _softplus(-lam_ref[:, _lanes(g)])
    a = jnp.exp(log_a)
    th = jnp.tanh(log_a)
    mult = jnp.sqrt(-2.0 * th / (1.0 - th))
    return a, xc * gate_i * mult


SCAN_ROWS = SUBLANES * SUBLANES


def _mix_prompt_kernel(u_ref, v_ref, cx_ref, cg_ref, ws_ref, bias_ref, cw_ref, cb_ref,
                       wa_ref, ba_ref, wi_ref, bi_ref, lam_ref,
                       ya_ref, yc_ref, h_ref,
                       xbuf, a_scr, x_scr, hs_scr, h_scr, *, tt):
    t_idx = pl.program_id(1)

    @pl.when(t_idx == 0)
    def _():
        xbuf[0:SUBLANES, :] = jnp.zeros((SUBLANES, xbuf.shape[1]), F32)
        h_scr[...] = jnp.zeros(h_scr.shape, F32)

    r_i = lax.broadcasted_iota(jnp.int32, (CHUNK, CHUNK), 0)
    c_i = lax.broadcasted_iota(jnp.int32, (CHUNK, CHUNK), 1)
    tril = c_i <= r_i
    w_low = [jnp.where(tril, ws_ref[g], 0.0).astype(BF16) for g in range(HEADS)]
    for c in range(tt // CHUNK):
        rows = slice(c * CHUNK, (c + 1) * CHUNK)
        vc = v_ref[0, rows, :].astype(BF16)
        mixed = jnp.concatenate(
            [jnp.dot(w_low[g], vc[:, g * HEAD_DIM:(g + 1) * HEAD_DIM], preferred_element_type=F32)
             for g in range(HEADS)], axis=-1)
        ya_ref[0, rows, :] = u_ref[0, rows, :] * (mixed + bias_ref[...])

    xbuf[SUBLANES:SUBLANES + tt, :] = cx_ref[0]
    xc = cb_ref[...] + cw_ref[CONV_W - 1:CONV_W, :] * xbuf[SUBLANES:SUBLANES + tt, :]
    for d in range(1, CONV_W):
        xc = xc + cw_ref[CONV_W - 1 - d:CONV_W - d, :] * xbuf[SUBLANES - d:SUBLANES - d + tt, :]
    xbuf[0:SUBLANES, :] = xbuf[tt:tt + SUBLANES, :]

    for g in range(HEADS):
        a, xin = _lru_coeffs(xc[:, _lanes(g)], g, wa_ref, ba_ref, wi_ref, bi_ref, lam_ref)
        a_scr[g] = a
        x_scr[g] = xin
        h = h_scr[0:1, _lanes(g)]
        for s in range(tt // SCAN_ROWS):
            base = s * SCAN_ROWS
            p_run, s_run = [], []
            for j in range(SUBLANES):
                a_j = a_scr[g, pl.ds(base + j, SUBLANES, stride=SUBLANES), :]
                x_j = x_scr[g, pl.ds(base + j, SUBLANES, stride=SUBLANES), :]
                if j == 0:
                    p_run.append(a_j)
                    s_run.append(x_j)
                else:
                    p_run.append(p_run[-1] * a_j)
                    s_run.append(a_j * s_run[-1] + x_j)
            carry_in = []
            for m in range(SUBLANES):
                carry_in.append(h)
                h = s_run[-1][m:m + 1, :] + p_run[-1][m:m + 1, :] * h
            cin = jnp.concatenate(carry_in, axis=0)
            for j in range(SUBLANES):
                hs_scr[g, pl.ds(base + j, SUBLANES, stride=SUBLANES), :] = s_run[j] + p_run[j] * cin
        h_scr[:, _lanes(g)] = jnp.broadcast_to(h, (SUBLANES, HEAD_DIM))
    h_ref[0] = h_scr[...]
    yc_ref[0] = jnp.concatenate([hs_scr[g] for g in range(HEADS)], axis=-1) * cg_ref[0]


def _mix_prompt(u, v, cx, cg, ws, bias_full, cw, cb, wa, ba, wi, bi, lam, *, tt):
    b, t, w = u.shape
    seq = pl.BlockSpec((1, tt, w), lambda i, j: (i, j, 0))
    ins = [u, v, cx, cg, ws, bias_full, cw, cb, wa, ba, wi, bi, lam]
    slab = pltpu.VMEM((HEADS, tt, HEAD_DIM), F32)
    return pl.pallas_call(
        functools.partial(_mix_prompt_kernel, tt=tt),
        grid=(b, t // tt),
        in_specs=[seq] * 4 + [_const_spec(a.shape) for a in ins[4:]],
        out_specs=[seq, seq, pl.BlockSpec((1, SUBLANES, w), lambda i, j: (i, 0, 0))],
        out_shape=(jax.ShapeDtypeStruct((b, t, w), F32), jax.ShapeDtypeStruct((b, t, w), F32),
                   jax.ShapeDtypeStruct((b, SUBLANES, w), F32)),
        scratch_shapes=[pltpu.VMEM((tt + SUBLANES, w), F32), slab, slab, slab,
                        pltpu.VMEM((SUBLANES, w), F32)],
        compiler_params=_cparams("arbitrary", "arbitrary"),
        name="mix_prompt",
    )(*ins)


def _mix_sample_kernel(u_ref, v_ref, cx_ref, cg_ref, wrow_ref, brow_ref, cw_ref, cb_ref,
                       wa_ref, ba_ref, wi_ref, bi_ref, lam_ref, h0_ref, conv0_ref,
                       ya_ref, yc_ref, h_ref, in_scr, out_scr, *, steps, nseq):
    w = u_ref.shape[1]
    for n, ref in enumerate((u_ref, v_ref, cx_ref, cg_ref)):
        for g in range(HEADS):
            in_scr[n * HEADS + g] = ref[:, _lanes(g)]

    def at_step(n, g, t):
        return in_scr[n * HEADS + g, pl.ds(t, nseq, stride=steps), :]

    for g in range(HEADS):
        v_t = [at_step(1, g, t) for t in range(steps)]
        for t in range(steps):
            mixed = brow_ref[t:t + 1, _lanes(g)]
            for s in range(t + 1):
                mixed = mixed + wrow_ref[t * steps + s:t * steps + s + 1, _lanes(g)] * v_t[s]
            out_scr[g, pl.ds(t, nseq, stride=steps), :] = at_step(0, g, t) * mixed

        xp = [conv0_ref[:, j * w + g * HEAD_DIM:j * w + (g + 1) * HEAD_DIM] for j in range(CONV_W - 1)]
        xp += [at_step(2, g, t) for t in range(steps)]
        h = h0_ref[:, _lanes(g)]
        for t in range(steps):
            xc = cb_ref[:, _lanes(g)]
            for j in range(CONV_W):
                xc = xc + cw_ref[j:j + 1, _lanes(g)] * xp[t + j]
            a, xin = _lru_coeffs(xc, g, wa_ref, ba_ref, wi_ref, bi_ref, lam_ref)
            h = a * h + xin
            out_scr[HEADS + g, pl.ds(t, nseq, stride=steps), :] = h * at_step(3, g, t)
        h_ref[:, _lanes(g)] = h
    ya_ref[...] = jnp.concatenate([out_scr[g] for g in range(HEADS)], axis=-1)
    yc_ref[...] = jnp.concatenate([out_scr[HEADS + g] for g in range(HEADS)], axis=-1)


def _mix_sample(u, v, cx, cg, wrow, brow, cw, cb, wa, ba, wi, bi, lam, h0, conv0, *, steps):
    n, w = u.shape
    nseq = n // steps
    ins = [u, v, cx, cg, wrow, brow, cw, cb, wa, ba, wi, bi, lam, h0, conv0]
    return pl.pallas_call(
        functools.partial(_mix_sample_kernel, steps=steps, nseq=nseq),
        grid=(1,),
        in_specs=[_const_spec(a.shape) for a in ins],
        out_specs=[_const_spec((n, w)), _const_spec((n, w)), _const_spec((nseq, w))],
        out_shape=(jax.ShapeDtypeStruct((n, w), F32), jax.ShapeDtypeStruct((n, w), F32),
                   jax.ShapeDtypeStruct((nseq, w), F32)),
        scratch_shapes=[pltpu.VMEM((4 * HEADS, n, HEAD_DIM), F32),
                        pltpu.VMEM((2 * HEADS, n, HEAD_DIM), F32)],
        compiler_params=_cparams("arbitrary"),
        name="mix_sample",
    )(*ins)


def _topk_mask(g, idx, n, axis):
    sel = jnp.zeros(g.shape, jnp.bool_)
    for _ in range(MOBA_TOPK):
        best = jnp.max(g, axis=axis, keepdims=True)
        first = jnp.min(jnp.where(g == best, idx, n), axis=axis, keepdims=True)
        pick = idx == first
        sel = jnp.logical_or(sel, pick)
        g = jnp.where(pick, -jnp.inf, g)
    return sel


SUM_ROWS = 16


def _attn_prompt_kernel(qt_ref, kb_ref, vt_ref, km_ref, o_ref, bias_scr, *, nb, group):
    blk = MOBA_BLOCK
    span = group * blk
    i = pl.program_id(1)
    own = pl.multiple_of(i * blk, blk)
    blk_i = lax.broadcasted_iota(jnp.int32, (nb, blk), 0)
    past = blk_i < i
    key_i = lax.broadcasted_iota(jnp.int32, (blk, blk), 0)
    qry_i = lax.broadcasted_iota(jnp.int32, (blk, blk), 1)

    def values(h, start, size):
        ones = jnp.ones((SUM_ROWS, size), BF16)
        return jnp.concatenate([vt_ref[_lanes(h), pl.ds(start, size)], ones], axis=0)

    qs = []
    for h in range(HEADS):
        qt = qt_ref[_lanes(h), :]
        gate = jnp.dot(km_ref[0, :, _lanes(h)], qt, precision=lax.Precision.HIGHEST,
                       preferred_element_type=F32)
        sel = jnp.logical_and(_topk_mask(jnp.where(past, gate, -jnp.inf), blk_i, nb, 0), past)
        sel_bias = jnp.where(sel, 0.0, NEG)
        for sj in range(nb // group):
            bias_scr[h, sj, 0:group, :] = sel_bias[sj * group:(sj + 1) * group, :]
        qs.append((qt * (HEAD_DIM ** -0.5 * LOG2_E)).astype(BF16))

    def logits(h, start, size):
        return jnp.dot(kb_ref[0, pl.ds(start, size), _lanes(h)], qs[h], preferred_element_type=F32)

    m0, acc0 = [], []
    s_next = logits(0, own, blk)
    for h in range(HEADS):
        s = jnp.where(key_i <= qry_i, s_next, NEG)
        if h + 1 < HEADS:
            s_next = logits(h + 1, own, blk)
        m0.append(jnp.max(s, axis=0, keepdims=True))
        acc0.append(jnp.dot(values(h, own, blk), jnp.exp2(s - m0[h]).astype(BF16),
                            preferred_element_type=F32))

    def body(sj, carry):
        start = pl.multiple_of(sj * span, span)

        out = []
        s_next = logits(0, start, span)
        for h in range(HEADS):
            m, acc = carry[h]
            s = s_next
            if h + 1 < HEADS:
                s_next = logits(h + 1, start, span)
            bias = bias_scr[h, sj, 0:group, :]
            s = (s.reshape(group, blk, blk) + bias[:, None, :]).reshape(span, blk)
            m_new = jnp.maximum(m, jnp.max(s, axis=0, keepdims=True))
            p = jnp.exp2(s - m_new).astype(BF16)
            acc = jnp.exp2(m - m_new) * acc + jnp.dot(values(h, start, span), p,
                                                      preferred_element_type=F32)
            out.append((m_new, acc))
        return tuple(out)

    fin = lax.fori_loop(0, (i + group - 1) // group, body, tuple(zip(m0, acc0)))
    o_ref[0] = jnp.concatenate(
        [(acc[:HEAD_DIM] / acc[HEAD_DIM:HEAD_DIM + 1]).T for _, acc in fin], axis=-1)


def _attn_prompt(qt, kb, vt, kmean):
    b, t, w = kb.shape
    nb = t // MOBA_BLOCK
    group = 4 if nb % 4 == 0 else 1
    return pl.pallas_call(
        functools.partial(_attn_prompt_kernel, nb=nb, group=group),
        grid=(b, nb),
        in_specs=[pl.BlockSpec((w, MOBA_BLOCK), lambda bi, i: (0, bi * nb + i)),
                  pl.BlockSpec((1, t, w), lambda bi, i: (bi, 0, 0)),
                  pl.BlockSpec((w, t), lambda bi, i: (0, bi)),
                  pl.BlockSpec((1, nb, w), lambda bi, i: (bi, 0, 0))],
        out_specs=pl.BlockSpec((1, MOBA_BLOCK, w), lambda bi, i: (bi, i, 0)),
        out_shape=jax.ShapeDtypeStruct((b, t, w), F32),
        scratch_shapes=[pltpu.VMEM((HEADS, nb // group, SUBLANES, MOBA_BLOCK), F32)],
        compiler_params=_cparams("arbitrary", "arbitrary"),
        name="attn_prompt",
    )(qt, kb, vt, kmean)


def _attn_sample_kernel(pt_ref, q_ref, kn_ref, vn_ref, *rest, n_pages, page, steps):
    del pt_ref
    kp = rest[:n_pages]
    vp = rest[n_pages:2 * n_pages]
    o_ref, kbuf, vbuf = rest[2 * n_pages:]
    past = n_pages * page
    nbp = past // MOBA_BLOCK
    per_blk = MOBA_BLOCK // page
    cols = LANES

    w = HEADS * HEAD_DIM
    sums = []
    for p in range(n_pages):
        kpg = jnp.concatenate([kp[p][0, 0, pl.ds(h, page, stride=HEADS), :] for h in range(HEADS)], axis=-1)
        vpg = jnp.concatenate([vp[p][0, 0, pl.ds(h, page, stride=HEADS), :] for h in range(HEADS)], axis=-1)
        sums.append(jnp.sum(kpg, axis=0, keepdims=True))
        kbuf[p * page:(p + 1) * page, :] = kpg.astype(BF16)
        vbuf[p * page:(p + 1) * page, :] = vpg.astype(BF16)
    pad = jnp.zeros((LANES - steps, w), F32)
    kbuf[past:past + LANES, :] = jnp.concatenate([kn_ref[0], pad], axis=0).astype(BF16)
    vbuf[past:past + LANES, :] = jnp.concatenate([vn_ref[0], pad], axis=0).astype(BF16)
    kmean = jnp.concatenate(
        [sum(sums[j * per_blk:(j + 1) * per_blk]) for j in range(nbp)], axis=0) * (1.0 / MOBA_BLOCK)

    q = q_ref[0]
    r_i = lax.broadcasted_iota(jnp.int32, (cols, w), 0)
    l_i = lax.broadcasted_iota(jnp.int32, (cols, w), 1)
    qmat = jnp.where(r_i // steps == l_i // HEAD_DIM, jnp.concatenate([q] * (cols // steps), axis=0), 0.0)
    gate = lax.dot_general(kmean, qmat, _NT, precision=lax.Precision.HIGHEST,
                           preferred_element_type=F32)
    s = lax.dot_general(kbuf[...], (qmat * (HEAD_DIM ** -0.5)).astype(BF16), _NT,
                        preferred_element_type=F32)

    blk_i = lax.broadcasted_iota(jnp.int32, (nbp, cols), 0)
    sel_bias = jnp.where(_topk_mask(gate, blk_i, nbp, 0), 0.0, NEG)
    s_past = (s[:past].reshape(nbp, MOBA_BLOCK, cols) + sel_bias[:, None, :]).reshape(past, cols)
    key_i = lax.broadcasted_iota(jnp.int32, (LANES, cols), 0)
    col_i = lax.broadcasted_iota(jnp.int32, (LANES, cols), 1)
    s_own = jnp.where(key_i <= col_i % steps, s[past:], NEG)
    m = jnp.maximum(jnp.max(s_past, axis=0, keepdims=True), jnp.max(s_own, axis=0, keepdims=True))
    p_t = jnp.concatenate([jnp.exp(s_past - m), jnp.exp(s_own - m)], axis=0).T
    l = jnp.sum(p_t, axis=1, keepdims=True)
    out = jnp.dot(p_t.astype(BF16), vbuf[...], preferred_element_type=F32) / l
    o_ref[0] = jnp.concatenate(
        [out[h * steps:(h + 1) * steps, _lanes(h)] for h in range(HEADS)], axis=-1)


def _attn_sample(page_table, q, k_new, v_new, cache_k, cache_v, layer):
    bs, steps, w = q.shape
    n_pages = page_table.shape[1]
    page = cache_k.shape[2] // HEADS
    new = pl.BlockSpec((1, steps, w), lambda b, pt: (b, 0, 0))

    def page_spec(p):
        return pl.BlockSpec((1, 1, page * HEADS, HEAD_DIM), lambda b, pt: (layer, pt[b, p], 0, 0))

    keys = n_pages * page + LANES
    return pl.pallas_call(
        functools.partial(_attn_sample_kernel, n_pages=n_pages, page=page, steps=steps),
        grid_spec=pltpu.PrefetchScalarGridSpec(
            num_scalar_prefetch=1,
            grid=(bs,),
            in_specs=[new, new, new] + [page_spec(p) for p in range(n_pages)] * 2,
            out_specs=new,
            scratch_shapes=[pltpu.VMEM((keys, w), BF16), pltpu.VMEM((keys, w), BF16)],
        ),
        out_shape=jax.ShapeDtypeStruct((bs, steps, w), F32),
        compiler_params=_cparams("arbitrary"),
        name="attn_sample",
    )(page_table, q, k_new, v_new, *([cache_k] * n_pages), *([cache_v] * n_pages))


def _merge_kernel(x_ref, ya_ref, yb_ref, yc_ref, gpre_ref, wg_ref, wb_ref, wo_ref, gpost_ref,
                  o_ref, *, d, col0):
    x = x_ref[...]
    xn = (_rms(x) * gpre_ref[...]).astype(BF16)
    merged = None
    for n, y_ref in enumerate((ya_ref, yb_ref, yc_ref)):
        gate = jax.nn.sigmoid(jnp.dot(xn, wg_ref[:, col0 + n * d:col0 + (n + 1) * d], preferred_element_type=F32))
        proj = jnp.dot(y_ref[...].astype(BF16), wb_ref[n], preferred_element_type=F32)
        merged = gate * proj if merged is None else merged + gate * proj
    mix = jnp.dot(merged.astype(BF16), wo_ref[...], preferred_element_type=F32)
    o_ref[...] = x + _rms(mix) * gpost_ref[...]


def _merge(x, ya, yb, yc, gpre, wg, wb, wo, l, gpost, *, tm):
    n, d = x.shape
    w = ya.shape[1]
    row = lambda i: (i, 0)
    return pl.pallas_call(
        functools.partial(_merge_kernel, d=d, col0=wg.shape[2] - 3 * d),
        grid=(n // tm,),
        in_specs=[pl.BlockSpec((tm, d), row)] + [pl.BlockSpec((tm, w), row)] * 3
                 + [_const_spec(gpre.shape)] + [_layer_spec(a.shape, l) for a in (wg, wb, wo)]
                 + [_const_spec(gpost.shape)],
        out_specs=pl.BlockSpec((tm, d), row),
        out_shape=jax.ShapeDtypeStruct((n, d), F32),
        compiler_params=_cparams("parallel"),
        name="merge",
    )(x, ya, yb, yc, gpre, wg, wb, wo, gpost)


def _ffn_kernel(x_ref, gpre_ref, wgu_ref, wd_ref, gpost_ref, o_ref, *, d_ff):
    x = x_ref[...]
    xn = (_rms(x) * gpre_ref[...]).astype(BF16)
    g = jnp.dot(xn, wgu_ref[:, :d_ff], preferred_element_type=F32)
    u = jnp.dot(xn, wgu_ref[:, d_ff:], preferred_element_type=F32)
    act = (jax.nn.silu(g) * u).astype(BF16)
    f = jnp.dot(act, wd_ref[...], preferred_element_type=F32)
    o_ref[...] = x + _rms(f) * gpost_ref[...]


def _ffn(x, gpre, wgu, wd, l, gpost, *, tm):
    n, d = x.shape
    row = lambda i: (i, 0)
    return pl.pallas_call(
        functools.partial(_ffn_kernel, d_ff=wd.shape[1]),
        grid=(n // tm,),
        in_specs=[pl.BlockSpec((tm, d), row), _const_spec(gpre.shape), _layer_spec(wgu.shape, l),
                  _layer_spec(wd.shape, l), _const_spec(gpost.shape)],
        out_specs=pl.BlockSpec((tm, d), row),
        out_shape=jax.ShapeDtypeStruct((n, d), F32),
        compiler_params=_cparams("parallel"),
        name="ffn",
    )(x, gpre, wgu, wd, gpost)


def _row_tile(n):
    for tm in (512, 256, 128, 64, 32, 16, 8):
        if n % tm == 0:
            return tm
    raise ValueError(f"row count {n} is not a multiple of {SUBLANES}")


def _layer_weights(l, P):
    row = lambda a: a[l].reshape(1, -1)
    return dict(
        g_mix_pre=row(P['norm_mix_pre']), g_mix_post=row(P['norm_mix_post']),
        g_ffn_pre=row(P['norm_ffn_pre']), g_ffn_post=row(P['norm_ffn_post']),
        l=l, w_in=P['w_in'], gv=row(P['a_norm_v']),
        ws=P['a_w_s'][l], bs=P['a_b_s'][l],
        cw=P['c_conv_w'][l], cb=row(P['c_conv_b']),
        wa=P['c_w_a'][l].astype(BF16), ba=row(P['c_b_a']),
        wi=P['c_w_i'][l].astype(BF16), bi=row(P['c_b_i']), lam=row(P['c_lambda']),
        wb=P['w_branch'], wo=P['w_out'], wgu=P['ffn_w_gu'], wd=P['ffn_w_down'])


def _finish_layer(x, ya, yb, yc, L, tm):
    x1 = _merge(x, ya, yb, yc, L['g_mix_pre'], L['w_in'], L['wb'], L['wo'], L['l'], L['g_mix_post'], tm=tm)
    return _ffn(x1, L['g_ffn_pre'], L['wgu'], L['wd'], L['l'], L['g_ffn_post'], tm=tm)


def _prompt_layer(x, L, cos, sin):
    b, t, d = x.shape
    width = L['gv'].shape[1]
    tm = _row_tile(t)
    xf = x.reshape(b * t, d)
    u, v, cx, cg, k_rows, v_rows, kb, qt, vt, kmean = _inproj(
        xf, L['g_mix_pre'], L['w_in'], L['l'], L['gv'], cos, sin, tm=tm, table_tiles=t // tm, prompt=True)
    seq = lambda a: a.reshape(b, t, width)
    bias_full = jnp.repeat(L['bs'][:, :CHUNK].T, HEAD_DIM, axis=1)
    ya, yc, h_last = _mix_prompt(seq(u), seq(v), seq(cx), seq(cg), L['ws'][:, :CHUNK, :CHUNK], bias_full,
                                 L['cw'], L['cb'], L['wa'], L['ba'], L['wi'], L['bi'], L['lam'],
                                 tt=_row_tile(t))
    yb = _attn_prompt(qt, seq(kb), vt, kmean.reshape(b, t // MOBA_BLOCK, width))
    y = _finish_layer(xf, ya.reshape(b * t, width), yb.reshape(b * t, width),
                      yc.reshape(b * t, width), L, tm)
    heads = lambda a: a.reshape(b, t, HEADS, HEAD_DIM)
    return (y.reshape(b, t, d), heads(k_rows), heads(v_rows), h_last[:, 0],
            seq(cx)[:, t - (CONV_W - 1):], seq(v)[:, t - CHUNK:])


def _sample_layer(l, x, L, cos, sin, page_table, cache_k, cache_v, h0, conv0):
    bs, steps, d = x.shape
    width = L['gv'].shape[1]
    n = bs * steps
    tm = _row_tile(n)
    xf = x.reshape(n, d)
    u, v, cx, cg, q, k, vv = _inproj(xf, L['g_mix_pre'], L['w_in'], L['l'], L['gv'], cos, sin,
                                     tm=tm, table_tiles=1, prompt=False)
    w_low = jnp.tril(L['ws'][:, :steps, :steps])
    wrow = jnp.repeat(w_low.transpose(1, 2, 0).reshape(steps * steps, HEADS), HEAD_DIM, axis=1)
    brow = jnp.repeat(L['bs'][:, :steps].T, HEAD_DIM, axis=1)
    ya, yc, h_new = _mix_sample(u, v, cx, cg, wrow, brow, L['cw'], L['cb'], L['wa'], L['ba'],
                                L['wi'], L['bi'], L['lam'], h0, conv0.reshape(bs, -1), steps=steps)
    seq = lambda a: a.reshape(bs, steps, width)
    yb = _attn_sample(page_table, seq(q), seq(k), seq(vv), cache_k, cache_v, l)
    y = _finish_layer(xf, ya, yb.reshape(n, width), yc, L, tm)
    conv_new = jnp.concatenate([conv0, seq(cx)], axis=1)[:, -(CONV_W - 1):]
    heads = lambda a: a.reshape(bs, steps, HEADS, HEAD_DIM)
    return y.reshape(bs, steps, d), heads(k), heads(vv), h_new, conv_new, seq(v)


def kernel(x_prompt, x_sample, cache_k, cache_v, page_table, state_lru_h, state_conv, norm_mix_pre, norm_mix_post, norm_ffn_pre, norm_ffn_post, w_in, a_norm_v, a_w_s, a_b_s, c_conv_w, c_conv_b, c_w_a, c_b_a, c_w_i, c_b_i, c_lambda, w_branch, w_out, ffn_w_gu, ffn_w_down):
    P = dict(norm_mix_pre=norm_mix_pre, norm_mix_post=norm_mix_post, norm_ffn_pre=norm_ffn_pre,
             norm_ffn_post=norm_ffn_post, w_in=w_in, a_norm_v=a_norm_v, a_w_s=a_w_s, a_b_s=a_b_s,
             c_conv_w=c_conv_w, c_conv_b=c_conv_b, c_w_a=c_w_a, c_b_a=c_b_a, c_w_i=c_w_i,
             c_b_i=c_b_i, c_lambda=c_lambda, w_branch=w_branch, w_out=w_out,
             ffn_w_gu=ffn_w_gu, ffn_w_down=ffn_w_down)
    for name in ('w_in', 'w_branch', 'w_out', 'ffn_w_gu', 'ffn_w_down'):
        P[name] = P[name].astype(BF16)
    depth = w_in.shape[0]
    tp = x_prompt.shape[1]
    bs, steps, _ = x_sample.shape
    n_pages, page = page_table.shape[1], cache_k.shape[2]
    past = n_pages * page
    assert past % MOBA_BLOCK == 0 and MOBA_BLOCK % page == 0 and tp % MOBA_BLOCK == 0
    assert cache_k.shape[3:] == (HEADS, HEAD_DIM)
    ck = cache_k.reshape(cache_k.shape[0], cache_k.shape[1], page * HEADS, HEAD_DIM)
    cv = cache_v.reshape(ck.shape)

    cos_p, sin_p = _rope_tables(tp, 0, tp)
    cos_s, sin_s = _rope_tables(_row_tile(bs * steps), past, steps)

    yp, ys = x_prompt, x_sample
    outs = [[] for _ in range(10)]
    for l in range(depth):
        L = _layer_weights(l, P)
        yp, k1, v1, h1, c1, a1 = _prompt_layer(yp, L, cos_p, sin_p)
        ys, k2, v2, h2, c2, a2 = _sample_layer(l, ys, L, cos_s, sin_s, page_table, ck, cv,
                                               state_lru_h[l], state_conv[l])
        for lst, val in zip(outs, (k1, v1, k2, v2, h1, h2, c1, c2, a1, a2)):
            lst.append(val)
    return (yp, ys) + tuple(jnp.stack(o) for o in outs)
```

```python
import functools

import jax
import jax.numpy as jnp
from jax import lax
from jax.experimental import pallas as pl
from jax.experimental.pallas import tpu as pltpu

F32 = jnp.float32
BF16 = jnp.bfloat16

EPS = 1e-6
ROPE_THETA = 10000.0
LRU_C = 8.0
HEADS = 4
HEAD_DIM = 128
CHUNK = 128
MOBA_BLOCK = 256
MOBA_TOPK = 3
CONV_W = 4
NEG = -1e30
LOG2_E = 1.4426950408889634
LANES = 128
SUBLANES = 8
VMEM_LIMIT = 56 * 1024 * 1024

_NT = (((1,), (1,)), ((), ()))


def _cparams(*sem):
    return pltpu.CompilerParams(dimension_semantics=sem, vmem_limit_bytes=VMEM_LIMIT)


def _const_spec(shape):
    zeros = (0,) * len(shape)
    return pl.BlockSpec(shape, lambda *_: zeros, pipeline_mode=pl.Buffered(1))


def _layer_spec(stacked_shape, l):
    zeros = (0,) * (len(stacked_shape) - 1)
    return pl.BlockSpec((None,) + tuple(stacked_shape[1:]), lambda *_: (l,) + zeros,
                        pipeline_mode=pl.Buffered(1))


def _rms(x):
    return x * lax.rsqrt(jnp.mean(x * x, axis=-1, keepdims=True) + EPS)


def _rope_table_kernel(inv_ref, cos_ref, sin_ref, *, pos0, period):
    n = cos_ref.shape[0]
    row = lax.broadcasted_iota(jnp.int32, (n, HEAD_DIM), 0)
    lane = lax.broadcasted_iota(jnp.int32, (n, HEAD_DIM), 1)
    pos = (pos0 + row % period).astype(F32)
    ang = pos * inv_ref[...]
    s = jnp.sin(ang)
    cos_ref[...] = jnp.cos(ang)
    sin_ref[...] = jnp.where(lane < HEAD_DIM // 2, -s, s)


def _rope_tables(n_rows, pos0, period):
    half = HEAD_DIM // 2
    inv = ROPE_THETA ** (-jnp.arange(half, dtype=F32) / half)
    inv = jnp.concatenate([inv, inv]).reshape(1, HEAD_DIM)
    return pl.pallas_call(
        functools.partial(_rope_table_kernel, pos0=pos0, period=period),
        out_shape=(jax.ShapeDtypeStruct((n_rows, HEAD_DIM), F32),) * 2,
        name="rope_tables",
    )(inv)


def _rope(z, cos, sin):
    outs = []
    for h in range(HEADS):
        zh = z[:, h * HEAD_DIM:(h + 1) * HEAD_DIM]
        outs.append(zh * cos + pltpu.roll(zh, HEAD_DIM // 2, 1) * sin)
    return jnp.concatenate(outs, axis=-1)


def _lanes(g):
    return slice(g * HEAD_DIM, (g + 1) * HEAD_DIM)


def _store_head_rows(ref, z, tm):
    for h in range(HEADS):
        ref[pl.ds(h, tm, stride=HEADS), :] = z[:, _lanes(h)]


def _inproj_kernel(x_ref, g_ref, w_ref, gv_ref, cos_ref, sin_ref, *outs, width, prompt):
    xn = (_rms(x_ref[...]) * g_ref[...]).astype(BF16)
    tm = x_ref.shape[0]

    def proj(c):
        return jnp.dot(xn, w_ref[:, c * width:(c + 1) * width], preferred_element_type=F32)

    if prompt:
        u_ref, v_ref, cx_ref, cg_ref, ko_ref, vo_ref, kb_ref, qt_ref, vt_ref, km_ref = outs
    else:
        u_ref, v_ref, cx_ref, cg_ref, q_ref, k_ref, vv_ref = outs
    u_ref[...] = jax.nn.gelu(proj(0))
    v_ref[...] = _rms(jax.nn.gelu(proj(1))) * gv_ref[...]
    cx_ref[...] = proj(5)
    cg_ref[...] = jax.nn.gelu(proj(6))
    cos = cos_ref[...]
    sin = sin_ref[...]
    q = _rope(proj(2), cos, sin)
    k = _rope(proj(3), cos, sin)
    vv = proj(4)
    if prompt:
        qt_ref[...] = q.T
        vt_ref[...] = vv.T.astype(BF16)
        kb_ref[...] = k.astype(BF16)
        for j in range(tm // MOBA_BLOCK):
            kj = k[j * MOBA_BLOCK:(j + 1) * MOBA_BLOCK]
            km_ref[j] = jnp.sum(kj, axis=0, keepdims=True) * (1.0 / MOBA_BLOCK)
        _store_head_rows(ko_ref, k, tm)
        _store_head_rows(vo_ref, vv, tm)
    else:
        q_ref[...] = q
        k_ref[...] = k
        vv_ref[...] = vv


def _inproj(x, g, w_in, l, gv, cos, sin, *, tm, table_tiles, prompt):
    n, d = x.shape
    width = gv.shape[1]
    row = lambda i: (i, 0)
    col = lambda i: (0, i)
    tab = lambda i: (i % table_tiles, 0)
    wide = (jax.ShapeDtypeStruct((n, width), F32), pl.BlockSpec((tm, width), row))
    if prompt:
        heads = (jax.ShapeDtypeStruct((n * HEADS, HEAD_DIM), F32), pl.BlockSpec((tm * HEADS, HEAD_DIM), row))
        nblk = tm // MOBA_BLOCK
        outs = [wide] * 4 + [heads] * 2 + [
            (jax.ShapeDtypeStruct((n, width), BF16), pl.BlockSpec((tm, width), row)),
            (jax.ShapeDtypeStruct((width, n), F32), pl.BlockSpec((width, tm), col)),
            (jax.ShapeDtypeStruct((width, n), BF16), pl.BlockSpec((width, tm), col)),
            (jax.ShapeDtypeStruct((n // MOBA_BLOCK, 1, width), F32),
             pl.BlockSpec((nblk, 1, width), lambda i: (i, 0, 0)))]
    else:
        outs = [wide] * 7
    return pl.pallas_call(
        functools.partial(_inproj_kernel, width=width, prompt=prompt),
        grid=(n // tm,),
        in_specs=[pl.BlockSpec((tm, d), row), _const_spec((1, d)),
                  _layer_spec((w_in.shape[0], d, 7 * width), l),
                  _const_spec((1, width)), pl.BlockSpec((tm, HEAD_DIM), tab),
                  pl.BlockSpec((tm, HEAD_DIM), tab)],
        out_specs=[o[1] for o in outs],
        out_shape=tuple(o[0] for o in outs),
        compiler_params=_cparams("parallel"),
        name="inproj_prompt" if prompt else "inproj_sample",
    )(x, g, w_in, gv, cos, sin)


def _softplus(x):
    return jnp.maximum(x, 0.0) + jnp.log1p(jnp.exp(-jnp.abs(x)))


def _lru_coeffs(xc, g, wa_ref, ba_ref, wi_ref, bi_ref, lam_ref):
    xb = xc.astype(BF16)
    r = jax.nn.sigmoid(jnp.dot(xb, wa_ref[g], preferred_element_type=F32) + ba_ref[:, _lanes(g)])
    gate_i = jax.nn.sigmoid(jnp.dot(xb, wi_ref[g], preferred_element_type=F32) + bi_ref[:, _lanes(g)])
    log_a = (-LRU_C) * r * _softplus(-lam_ref[:, _lanes(g)])
    a = jnp.exp(log_a)
    th = jnp.tanh(log_a)
    mult = jnp.sqrt(-2.0 * th / (1.0 - th))
    return a, xc * gate_i * mult


SCAN_ROWS = SUBLANES * SUBLANES


def _mix_prompt_kernel(u_ref, v_ref, cx_ref, cg_ref, ws_ref, bias_ref, cw_ref, cb_ref,
                       wa_ref, ba_ref, wi_ref, bi_ref, lam_ref,
                       ya_ref, yc_ref, h_ref,
                       xbuf, a_scr, x_scr, hs_scr, h_scr, *, tt):
    t_idx = pl.program_id(1)

    @pl.when(t_idx == 0)
    def _():
        xbuf[0:SUBLANES, :] = jnp.zeros((SUBLANES, xbuf.shape[1]), F32)
        h_scr[...] = jnp.zeros(h_scr.shape, F32)

    r_i = lax.broadcasted_iota(jnp.int32, (CHUNK, CHUNK), 0)
    c_i = lax.broadcasted_iota(jnp.int32, (CHUNK, CHUNK), 1)
    tril = c_i <= r_i
    w_low = [jnp.where(tril, ws_ref[g], 0.0).astype(BF16) for g in range(HEADS)]
    for c in range(tt // CHUNK):
        rows = slice(c * CHUNK, (c + 1) * CHUNK)
        vc = v_ref[0, rows, :].astype(BF16)
        mixed = jnp.concatenate(
            [jnp.dot(w_low[g], vc[:, g * HEAD_DIM:(g + 1) * HEAD_DIM], preferred_element_type=F32)
             for g in range(HEADS)], axis=-1)
        ya_ref[0, rows, :] = u_ref[0, rows, :] * (mixed + bias_ref[...])

    xbuf[SUBLANES:SUBLANES + tt, :] = cx_ref[0]
    xc = cb_ref[...] + cw_ref[CONV_W - 1:CONV_W, :] * xbuf[SUBLANES:SUBLANES + tt, :]
    for d in range(1, CONV_W):
        xc = xc + cw_ref[CONV_W - 1 - d:CONV_W - d, :] * xbuf[SUBLANES - d:SUBLANES - d + tt, :]
    xbuf[0:SUBLANES, :] = xbuf[tt:tt + SUBLANES, :]

    for g in range(HEADS):
        a, xin = _lru_coeffs(xc[:, _lanes(g)], g, wa_ref, ba_ref, wi_ref, bi_ref, lam_ref)
        a_scr[g] = a
        x_scr[g] = xin
        h = h_scr[0:1, _lanes(g)]
        for s in range(tt // SCAN_ROWS):
            base = s * SCAN_ROWS
            p_run, s_run = [], []
            for j in range(SUBLANES):
                a_j = a_scr[g, pl.ds(base + j, SUBLANES, stride=SUBLANES), :]
                x_j = x_scr[g, pl.ds(base + j, SUBLANES, stride=SUBLANES), :]
                if j == 0:
                    p_run.append(a_j)
                    s_run.append(x_j)
                else:
                    p_run.append(p_run[-1] * a_j)
                    s_run.append(a_j * s_run[-1] + x_j)
            carry_in = []
            for m in range(SUBLANES):
                carry_in.append(h)
                h = s_run[-1][m:m + 1, :] + p_run[-1][m:m + 1, :] * h
            cin = jnp.concatenate(carry_in, axis=0)
            for j in range(SUBLANES):
                hs_scr[g, pl.ds(base + j, SUBLANES, stride=SUBLANES), :] = s_run[j] + p_run[j] * cin
        h_scr[:, _lanes(g)] = jnp.broadcast_to(h, (SUBLANES, HEAD_DIM))
    h_ref[0] = h_scr[...]
    yc_ref[0] = jnp.concatenate([hs_scr[g] for g in range(HEADS)], axis=-1) * cg_ref[0]


def _mix_prompt(u, v, cx, cg, ws, bias_full, cw, cb, wa, ba, wi, bi, lam, *, tt):
    b, t, w = u.shape
    seq = pl.BlockSpec((1, tt, w), lambda i, j: (i, j, 0))
    ins = [u, v, cx, cg, ws, bias_full, cw, cb, wa, ba, wi, bi, lam]
    slab = pltpu.VMEM((HEADS, tt, HEAD_DIM), F32)
    return pl.pallas_call(
        functools.partial(_mix_prompt_kernel, tt=tt),
        grid=(b, t // tt),
        in_specs=[seq] * 4 + [_const_spec(a.shape) for a in ins[4:]],
        out_specs=[seq, seq, pl.BlockSpec((1, SUBLANES, w), lambda i, j: (i, 0, 0))],
        out_shape=(jax.ShapeDtypeStruct((b, t, w), F32), jax.ShapeDtypeStruct((b, t, w), F32),
                   jax.ShapeDtypeStruct((b, SUBLANES, w), F32)),
        scratch_shapes=[pltpu.VMEM((tt + SUBLANES, w), F32), slab, slab, slab,
                        pltpu.VMEM((SUBLANES, w), F32)],
        compiler_params=_cparams("arbitrary", "arbitrary"),
        name="mix_prompt",
    )(*ins)


def _mix_sample_kernel(u_ref, v_ref, cx_ref, cg_ref, wrow_ref, brow_ref, cw_ref, cb_ref,
                       wa_ref, ba_ref, wi_ref, bi_ref, lam_ref, h0_ref, conv0_ref,
                       ya_ref, yc_ref, h_ref, in_scr, out_scr, *, steps, nseq):
    w = u_ref.shape[1]
    for n, ref in enumerate((u_ref, v_ref, cx_ref, cg_ref)):
        for g in range(HEADS):
            in_scr[n * HEADS + g] = ref[:, _lanes(g)]

    def at_step(n, g, t):
        return in_scr[n * HEADS + g, pl.ds(t, nseq, stride=steps), :]

    for g in range(HEADS):
        v_t = [at_step(1, g, t) for t in range(steps)]
        for t in range(steps):
            mixed = brow_ref[t:t + 1, _lanes(g)]
            for s in range(t + 1):
                mixed = mixed + wrow_ref[t * steps + s:t * steps + s + 1, _lanes(g)] * v_t[s]
            out_scr[g, pl.ds(t, nseq, stride=steps), :] = at_step(0, g, t) * mixed

        xp = [conv0_ref[:, j * w + g * HEAD_DIM:j * w + (g + 1) * HEAD_DIM] for j in range(CONV_W - 1)]
        xp += [at_step(2, g, t) for t in range(steps)]
        h = h0_ref[:, _lanes(g)]
        for t in range(steps):
            xc = cb_ref[:, _lanes(g)]
            for j in range(CONV_W):
                xc = xc + cw_ref[j:j + 1, _lanes(g)] * xp[t + j]
            a, xin = _lru_coeffs(xc, g, wa_ref, ba_ref, wi_ref, bi_ref, lam_ref)
            h = a * h + xin
            out_scr[HEADS + g, pl.ds(t, nseq, stride=steps), :] = h * at_step(3, g, t)
        h_ref[:, _lanes(g)] = h
    ya_ref[...] = jnp.concatenate([out_scr[g] for g in range(HEADS)], axis=-1)
    yc_ref[...] = jnp.concatenate([out_scr[HEADS + g] for g in range(HEADS)], axis=-1)


def _mix_sample(u, v, cx, cg, wrow, brow, cw, cb, wa, ba, wi, bi, lam, h0, conv0, *, steps):
    n, w = u.shape
    nseq = n // steps
    ins = [u, v, cx, cg, wrow, brow, cw, cb, wa, ba, wi, bi, lam, h0, conv0]
    return pl.pallas_call(
        functools.partial(_mix_sample_kernel, steps=steps, nseq=nseq),
        grid=(1,),
        in_specs=[_const_spec(a.shape) for a in ins],
        out_specs=[_const_spec((n, w)), _const_spec((n, w)), _const_spec((nseq, w))],
        out_shape=(jax.ShapeDtypeStruct((n, w), F32), jax.ShapeDtypeStruct((n, w), F32),
                   jax.ShapeDtypeStruct((nseq, w), F32)),
        scratch_shapes=[pltpu.VMEM((4 * HEADS, n, HEAD_DIM), F32),
                        pltpu.VMEM((2 * HEADS, n, HEAD_DIM), F32)],
        compiler_params=_cparams("arbitrary"),
        name="mix_sample",
    )(*ins)


def _topk_mask(g, idx, n, axis):
    sel = jnp.zeros(g.shape, jnp.bool_)
    for _ in range(MOBA_TOPK):
        best = jnp.max(g, axis=axis, keepdims=True)
        first = jnp.min(jnp.where(g == best, idx, n), axis=axis, keepdims=True)
        pick = idx == first
        sel = jnp.logical_or(sel, pick)
        g = jnp.where(pick, -jnp.inf, g)
    return sel


SUM_ROWS = 16


QUERY_BLOCKS = 2


def _attn_prompt_kernel(qt_ref, kb_ref, vt_ref, km_ref, o_ref, bias_scr, *, nb, group):
    blk = MOBA_BLOCK
    span = group * blk
    nq = QUERY_BLOCKS * blk
    i0 = pl.program_id(1) * QUERY_BLOCKS
    own = pl.multiple_of(i0 * blk, nq)
    blk_i = lax.broadcasted_iota(jnp.int32, (nb, nq), 0)
    lane_i = lax.broadcasted_iota(jnp.int32, (nb, nq), 1)
    past = blk_i < i0 + lane_i // blk
    key_i = lax.broadcasted_iota(jnp.int32, (nq, nq), 0)
    qry_i = lax.broadcasted_iota(jnp.int32, (nq, nq), 1)
    routed_own = jnp.logical_and(key_i < blk, qry_i >= blk)

    def values(h, start, size):
        ones = jnp.ones((SUM_ROWS, size), BF16)
        return jnp.concatenate([vt_ref[_lanes(h), pl.ds(start, size)], ones], axis=0)

    qs, own_bias = [], []
    for h in range(HEADS):
        qt = qt_ref[_lanes(h), :]
        gate = jnp.dot(km_ref[0, :, _lanes(h)], qt, precision=lax.Precision.HIGHEST,
                       preferred_element_type=F32)
        sel = jnp.logical_and(_topk_mask(jnp.where(past, gate, -jnp.inf), blk_i, nb, 0), past)
        sel_bias = jnp.where(sel, 0.0, NEG)
        own_bias.append(jnp.sum(jnp.where(blk_i == i0, sel_bias, 0.0), axis=0, keepdims=True))
        loop_bias = jnp.where(blk_i < i0, sel_bias, NEG)
        for sj in range(nb // group):
            bias_scr[h, sj, 0:group, :] = loop_bias[sj * group:(sj + 1) * group, :]
        qs.append((qt * (HEAD_DIM ** -0.5 * LOG2_E)).astype(BF16))

    def logits(h, start, size):
        return jnp.dot(kb_ref[0, pl.ds(start, size), _lanes(h)], qs[h], preferred_element_type=F32)

    m0, acc0 = [], []
    s_next = logits(0, own, nq)
    for h in range(HEADS):
        s = jnp.where(key_i <= qry_i, s_next + jnp.where(routed_own, own_bias[h], 0.0), NEG)
        if h + 1 < HEADS:
            s_next = logits(h + 1, own, nq)
        m0.append(jnp.max(s, axis=0, keepdims=True))
        acc0.append(jnp.dot(values(h, own, nq), jnp.exp2(s - m0[h]).astype(BF16),
                            preferred_element_type=F32))

    def body(sj, carry):
        start = pl.multiple_of(sj * span, span)

        out = []
        s_next = logits(0, start, span)
        for h in range(HEADS):
            m, acc = carry[h]
            s = s_next
            if h + 1 < HEADS:
                s_next = logits(h + 1, start, span)
            bias = bias_scr[h, sj, 0:group, :]
            s = (s.reshape(group, blk, nq) + bias[:, None, :]).reshape(span, nq)
            m_new = jnp.maximum(m, jnp.max(s, axis=0, keepdims=True))
            p = jnp.exp2(s - m_new).astype(BF16)
            acc = jnp.exp2(m - m_new) * acc + jnp.dot(values(h, start, span), p,
                                                      preferred_element_type=F32)
            out.append((m_new, acc))
        return tuple(out)

    fin = lax.fori_loop(0, (i0 + group - 1) // group, body, tuple(zip(m0, acc0)))
    o_ref[0] = jnp.concatenate(
        [(acc[:HEAD_DIM] / acc[HEAD_DIM:HEAD_DIM + 1]).T for _, acc in fin], axis=-1)


def _attn_prompt(qt, kb, vt, kmean):
    b, t, w = kb.shape
    nb = t // MOBA_BLOCK
    nq = QUERY_BLOCKS * MOBA_BLOCK
    steps = nb // QUERY_BLOCKS
    group = 4 if nb % 4 == 0 else QUERY_BLOCKS
    whole = dict(pipeline_mode=pl.Buffered(1))
    return pl.pallas_call(
        functools.partial(_attn_prompt_kernel, nb=nb, group=group),
        grid=(b, steps),
        in_specs=[pl.BlockSpec((w, nq), lambda bi, i: (0, bi * steps + i)),
                  pl.BlockSpec((1, t, w), lambda bi, i: (bi, 0, 0), **whole),
                  pl.BlockSpec((w, t), lambda bi, i: (0, bi), **whole),
                  pl.BlockSpec((1, nb, w), lambda bi, i: (bi, 0, 0), **whole)],
        out_specs=pl.BlockSpec((1, nq, w), lambda bi, i: (bi, i, 0)),
        out_shape=jax.ShapeDtypeStruct((b, t, w), F32),
        scratch_shapes=[pltpu.VMEM((HEADS, nb // group, SUBLANES, nq), F32)],
        compiler_params=_cparams("arbitrary", "arbitrary"),
        name="attn_prompt",
    )(qt, kb, vt, kmean)


def _attn_sample_kernel(pt_ref, q_ref, kn_ref, vn_ref, *rest, n_pages, page, steps):
    del pt_ref
    kp = rest[:n_pages]
    vp = rest[n_pages:2 * n_pages]
    o_ref, kbuf, vbuf = rest[2 * n_pages:]
    past = n_pages * page
    nbp = past // MOBA_BLOCK
    per_blk = MOBA_BLOCK // page
    cols = LANES

    w = HEADS * HEAD_DIM
    sums = []
    for p in range(n_pages):
        kpg = jnp.concatenate([kp[p][0, 0, pl.ds(h, page, stride=HEADS), :] for h in range(HEADS)], axis=-1)
        vpg = jnp.concatenate([vp[p][0, 0, pl.ds(h, page, stride=HEADS), :] for h in range(HEADS)], axis=-1)
        sums.append(jnp.sum(kpg, axis=0, keepdims=True))
        kbuf[p * page:(p + 1) * page, :] = kpg.astype(BF16)
        vbuf[p * page:(p + 1) * page, :] = vpg.astype(BF16)
    pad = jnp.zeros((LANES - steps, w), F32)
    kbuf[past:past + LANES, :] = jnp.concatenate([kn_ref[0], pad], axis=0).astype(BF16)
    vbuf[past:past + LANES, :] = jnp.concatenate([vn_ref[0], pad], axis=0).astype(BF16)
    kmean = jnp.concatenate(
        [sum(sums[j * per_blk:(j + 1) * per_blk]) for j in range(nbp)], axis=0) * (1.0 / MOBA_BLOCK)

    q = q_ref[0]
    r_i = lax.broadcasted_iota(jnp.int32, (cols, w), 0)
    l_i = lax.broadcasted_iota(jnp.int32, (cols, w), 1)
    qmat = jnp.where(r_i // steps == l_i // HEAD_DIM, jnp.concatenate([q] * (cols // steps), axis=0), 0.0)
    gate = lax.dot_general(kmean, qmat, _NT, precision=lax.Precision.HIGHEST,
                           preferred_element_type=F32)
    s = lax.dot_general(kbuf[...], (qmat * (HEAD_DIM ** -0.5)).astype(BF16), _NT,
                        preferred_element_type=F32)

    blk_i = lax.broadcasted_iota(jnp.int32, (nbp, cols), 0)
    sel_bias = jnp.where(_topk_mask(gate, blk_i, nbp, 0), 0.0, NEG)
    s_past = (s[:past].reshape(nbp, MOBA_BLOCK, cols) + sel_bias[:, None, :]).reshape(past, cols)
    key_i = lax.broadcasted_iota(jnp.int32, (LANES, cols), 0)
    col_i = lax.broadcasted_iota(jnp.int32, (LANES, cols), 1)
    s_own = jnp.where(key_i <= col_i % steps, s[past:], NEG)
    m = jnp.maximum(jnp.max(s_past, axis=0, keepdims=True), jnp.max(s_own, axis=0, keepdims=True))
    p_t = jnp.concatenate([jnp.exp(s_past - m), jnp.exp(s_own - m)], axis=0).T
    l = jnp.sum(p_t, axis=1, keepdims=True)
    out = jnp.dot(p_t.astype(BF16), vbuf[...], preferred_element_type=F32) / l
    o_ref[0] = jnp.concatenate(
        [out[h * steps:(h + 1) * steps, _lanes(h)] for h in range(HEADS)], axis=-1)


def _attn_sample(page_table, q, k_new, v_new, cache_k, cache_v, layer):
    bs, steps, w = q.shape
    n_pages = page_table.shape[1]
    page = cache_k.shape[2] // HEADS
    new = pl.BlockSpec((1, steps, w), lambda b, pt: (b, 0, 0))

    def page_spec(p):
        return pl.BlockSpec((1, 1, page * HEADS, HEAD_DIM), lambda b, pt: (layer, pt[b, p], 0, 0))

    keys = n_pages * page + LANES
    return pl.pallas_call(
        functools.partial(_attn_sample_kernel, n_pages=n_pages, page=page, steps=steps),
        grid_spec=pltpu.PrefetchScalarGridSpec(
            num_scalar_prefetch=1,
            grid=(bs,),
            in_specs=[new, new, new] + [page_spec(p) for p in range(n_pages)] * 2,
            out_specs=new,
            scratch_shapes=[pltpu.VMEM((keys, w), BF16), pltpu.VMEM((keys, w), BF16)],
        ),
        out_shape=jax.ShapeDtypeStruct((bs, steps, w), F32),
        compiler_params=_cparams("arbitrary"),
        name="attn_sample",
    )(page_table, q, k_new, v_new, *([cache_k] * n_pages), *([cache_v] * n_pages))


def _merge_kernel(x_ref, ya_ref, yb_ref, yc_ref, gpre_ref, wg_ref, wb_ref, wo_ref, gpost_ref,
                  o_ref, *, d, col0):
    x = x_ref[...]
    xn = (_rms(x) * gpre_ref[...]).astype(BF16)
    merged = None
    for n, y_ref in enumerate((ya_ref, yb_ref, yc_ref)):
        gate = jax.nn.sigmoid(jnp.dot(xn, wg_ref[:, col0 + n * d:col0 + (n + 1) * d], preferred_element_type=F32))
        proj = jnp.dot(y_ref[...].astype(BF16), wb_ref[n], preferred_element_type=F32)
        merged = gate * proj if merged is None else merged + gate * proj
    mix = jnp.dot(merged.astype(BF16), wo_ref[...], preferred_element_type=F32)
    o_ref[...] = x + _rms(mix) * gpost_ref[...]


def _merge(x, ya, yb, yc, gpre, wg, wb, wo, l, gpost, *, tm):
    n, d = x.shape
    w = ya.shape[1]
    row = lambda i: (i, 0)
    return pl.pallas_call(
        functools.partial(_merge_kernel, d=d, col0=wg.shape[2] - 3 * d),
        grid=(n // tm,),
        in_specs=[pl.BlockSpec((tm, d), row)] + [pl.BlockSpec((tm, w), row)] * 3
                 + [_const_spec(gpre.shape)] + [_layer_spec(a.shape, l) for a in (wg, wb, wo)]
                 + [_const_spec(gpost.shape)],
        out_specs=pl.BlockSpec((tm, d), row),
        out_shape=jax.ShapeDtypeStruct((n, d), F32),
        compiler_params=_cparams("parallel"),
        name="merge",
    )(x, ya, yb, yc, gpre, wg, wb, wo, gpost)


def _ffn_kernel(x_ref, gpre_ref, wgu_ref, wd_ref, gpost_ref, o_ref, *, d_ff):
    x = x_ref[...]
    xn = (_rms(x) * gpre_ref[...]).astype(BF16)
    g = jnp.dot(xn, wgu_ref[:, :d_ff], preferred_element_type=F32)
    u = jnp.dot(xn, wgu_ref[:, d_ff:], preferred_element_type=F32)
    act = (jax.nn.silu(g) * u).astype(BF16)
    f = jnp.dot(act, wd_ref[...], preferred_element_type=F32)
    o_ref[...] = x + _rms(f) * gpost_ref[...]


def _ffn(x, gpre, wgu, wd, l, gpost, *, tm):
    n, d = x.shape
    row = lambda i: (i, 0)
    return pl.pallas_call(
        functools.partial(_ffn_kernel, d_ff=wd.shape[1]),
        grid=(n // tm,),
        in_specs=[pl.BlockSpec((tm, d), row), _const_spec(gpre.shape), _layer_spec(wgu.shape, l),
                  _layer_spec(wd.shape, l), _const_spec(gpost.shape)],
        out_specs=pl.BlockSpec((tm, d), row),
        out_shape=jax.ShapeDtypeStruct((n, d), F32),
        compiler_params=_cparams("parallel"),
        name="ffn",
    )(x, gpre, wgu, wd, gpost)


def _row_tile(n):
    for tm in (512, 256, 128, 64, 32, 16, 8):
        if n % tm == 0:
            return tm
    raise ValueError(f"row count {n} is not a multiple of {SUBLANES}")


def _layer_weights(l, P):
    row = lambda a: a[l].reshape(1, -1)
    return dict(
        g_mix_pre=row(P['norm_mix_pre']), g_mix_post=row(P['norm_mix_post']),
        g_ffn_pre=row(P['norm_ffn_pre']), g_ffn_post=row(P['norm_ffn_post']),
        l=l, w_in=P['w_in'], gv=row(P['a_norm_v']),
        ws=P['a_w_s'][l], bs=P['a_b_s'][l],
        cw=P['c_conv_w'][l], cb=row(P['c_conv_b']),
        wa=P['c_w_a'][l].astype(BF16), ba=row(P['c_b_a']),
        wi=P['c_w_i'][l].astype(BF16), bi=row(P['c_b_i']), lam=row(P['c_lambda']),
        wb=P['w_branch'], wo=P['w_out'], wgu=P['ffn_w_gu'], wd=P['ffn_w_down'])


def _finish_layer(x, ya, yb, yc, L, tm):
    x1 = _merge(x, ya, yb, yc, L['g_mix_pre'], L['w_in'], L['wb'], L['wo'], L['l'], L['g_mix_post'], tm=tm)
    return _ffn(x1, L['g_ffn_pre'], L['wgu'], L['wd'], L['l'], L['g_ffn_post'], tm=tm)


def _prompt_layer(x, L, cos, sin):
    b, t, d = x.shape
    width = L['gv'].shape[1]
    tm = _row_tile(t)
    xf = x.reshape(b * t, d)
    u, v, cx, cg, k_rows, v_rows, kb, qt, vt, kmean = _inproj(
        xf, L['g_mix_pre'], L['w_in'], L['l'], L['gv'], cos, sin, tm=tm, table_tiles=t // tm, prompt=True)
    seq = lambda a: a.reshape(b, t, width)
    bias_full = jnp.repeat(L['bs'][:, :CHUNK].T, HEAD_DIM, axis=1)
    ya, yc, h_last = _mix_prompt(seq(u), seq(v), seq(cx), seq(cg), L['ws'][:, :CHUNK, :CHUNK], bias_full,
                                 L['cw'], L['cb'], L['wa'], L['ba'], L['wi'], L['bi'], L['lam'],
                                 tt=_row_tile(t))
    yb = _attn_prompt(qt, seq(kb), vt, kmean.reshape(b, t // MOBA_BLOCK, width))
    y = _finish_layer(xf, ya.reshape(b * t, width), yb.reshape(b * t, width),
                      yc.reshape(b * t, width), L, tm)
    heads = lambda a: a.reshape(b, t, HEADS, HEAD_DIM)
    return (y.reshape(b, t, d), heads(k_rows), heads(v_rows), h_last[:, 0],
            seq(cx)[:, t - (CONV_W - 1):], seq(v)[:, t - CHUNK:])


def _sample_layer(l, x, L, cos, sin, page_table, cache_k, cache_v, h0, conv0):
    bs, steps, d = x.shape
    width = L['gv'].shape[1]
    n = bs * steps
    tm = _row_tile(n)
    xf = x.reshape(n, d)
    u, v, cx, cg, q, k, vv = _inproj(xf, L['g_mix_pre'], L['w_in'], L['l'], L['gv'], cos, sin,
                                     tm=tm, table_tiles=1, prompt=False)
    w_low = jnp.tril(L['ws'][:, :steps, :steps])
    wrow = jnp.repeat(w_low.transpose(1, 2, 0).reshape(steps * steps, HEADS), HEAD_DIM, axis=1)
    brow = jnp.repeat(L['bs'][:, :steps].T, HEAD_DIM, axis=1)
    ya, yc, h_new = _mix_sample(u, v, cx, cg, wrow, brow, L['cw'], L['cb'], L['wa'], L['ba'],
                                L['wi'], L['bi'], L['lam'], h0, conv0.reshape(bs, -1), steps=steps)
    seq = lambda a: a.reshape(bs, steps, width)
    yb = _attn_sample(page_table, seq(q), seq(k), seq(vv), cache_k, cache_v, l)
    y = _finish_layer(xf, ya, yb.reshape(n, width), yc, L, tm)
    conv_new = jnp.concatenate([conv0, seq(cx)], axis=1)[:, -(CONV_W - 1):]
    heads = lambda a: a.reshape(bs, steps, HEADS, HEAD_DIM)
    return y.reshape(bs, steps, d), heads(k), heads(vv), h_new, conv_new, seq(v)


def kernel(x_prompt, x_sample, cache_k, cache_v, page_table, state_lru_h, state_conv, norm_mix_pre, norm_mix_post, norm_ffn_pre, norm_ffn_post, w_in, a_norm_v, a_w_s, a_b_s, c_conv_w, c_conv_b, c_w_a, c_b_a, c_w_i, c_b_i, c_lambda, w_branch, w_out, ffn_w_gu, ffn_w_down):
    P = dict(norm_mix_pre=norm_mix_pre, norm_mix_post=norm_mix_post, norm_ffn_pre=norm_ffn_pre,
             norm_ffn_post=norm_ffn_post, w_in=w_in, a_norm_v=a_norm_v, a_w_s=a_w_s, a_b_s=a_b_s,
             c_conv_w=c_conv_w, c_conv_b=c_conv_b, c_w_a=c_w_a, c_b_a=c_b_a, c_w_i=c_w_i,
             c_b_i=c_b_i, c_lambda=c_lambda, w_branch=w_branch, w_out=w_out,
             ffn_w_gu=ffn_w_gu, ffn_w_down=ffn_w_down)
    for name in ('w_in', 'w_branch', 'w_out', 'ffn_w_gu', 'ffn_w_down'):
        P[name] = P[name].astype(BF16)
    depth = w_in.shape[0]
    tp = x_prompt.shape[1]
    bs, steps, _ = x_sample.shape
    n_pages, page = page_table.shape[1], cache_k.shape[2]
    past = n_pages * page
    assert past % MOBA_BLOCK == 0 and MOBA_BLOCK % page == 0 and tp % (QUERY_BLOCKS * MOBA_BLOCK) == 0
    assert cache_k.shape[3:] == (HEADS, HEAD_DIM)
    ck = cache_k.reshape(cache_k.shape[0], cache_k.shape[1], page * HEADS, HEAD_DIM)
    cv = cache_v.reshape(ck.shape)

    cos_p, sin_p = _rope_tables(tp, 0, tp)
    cos_s, sin_s = _rope_tables(_row_tile(bs * steps), past, steps)

    yp, ys = x_prompt, x_sample
    outs = [[] for _ in range(10)]
    for l in range(depth):
        L = _layer_weights(l, P)
        yp, k1, v1, h1, c1, a1 = _prompt_layer(yp, L, cos_p, sin_p)
        ys, k2, v2, h2, c2, a2 = _sample_layer(l, ys, L, cos_s, sin_s, page_table, ck, cv,
                                               state_lru_h[l], state_conv[l])
        for lst, val in zip(outs, (k1, v1, k2, v2, h1, h2, c1, c2, a1, a2)):
            lst.append(val)
    return (yp, ys) + tuple(jnp.stack(o) for o in outs)
```

```python
import functools

import jax
import jax.numpy as jnp
from jax import lax
from jax.experimental import pallas as pl
from jax.experimental.pallas import tpu as pltpu

F32 = jnp.float32
BF16 = jnp.bfloat16

EPS = 1e-6
ROPE_THETA = 10000.0
LRU_C = 8.0
HEADS = 4
HEAD_DIM = 128
CHUNK = 128
MOBA_BLOCK = 256
MOBA_TOPK = 3
CONV_W = 4
NEG = -1e30
LOG2_E = 1.4426950408889634
LANES = 128
SUBLANES = 8
VMEM_LIMIT = 56 * 1024 * 1024

_NT = (((1,), (1,)), ((), ()))


def _cparams(*sem):
    return pltpu.CompilerParams(dimension_semantics=sem, vmem_limit_bytes=VMEM_LIMIT)


def _const_spec(shape):
    zeros = (0,) * len(shape)
    return pl.BlockSpec(shape, lambda *_: zeros, pipeline_mode=pl.Buffered(1))


def _layer_spec(stacked_shape, l):
    zeros = (0,) * (len(stacked_shape) - 1)
    return pl.BlockSpec((None,) + tuple(stacked_shape[1:]), lambda *_: (l,) + zeros,
                        pipeline_mode=pl.Buffered(1))


def _rms(x):
    return x * lax.rsqrt(jnp.mean(x * x, axis=-1, keepdims=True) + EPS)


def _rope_table_kernel(inv_ref, cos_ref, sin_ref, *, pos0, period):
    n = cos_ref.shape[0]
    row = lax.broadcasted_iota(jnp.int32, (n, HEAD_DIM), 0)
    lane = lax.broadcasted_iota(jnp.int32, (n, HEAD_DIM), 1)
    pos = (pos0 + row % period).astype(F32)
    ang = pos * inv_ref[...]
    s = jnp.sin(ang)
    cos_ref[...] = jnp.cos(ang)
    sin_ref[...] = jnp.where(lane < HEAD_DIM // 2, -s, s)


def _rope_tables(n_rows, pos0, period):
    half = HEAD_DIM // 2
    inv = ROPE_THETA ** (-jnp.arange(half, dtype=F32) / half)
    inv = jnp.concatenate([inv, inv]).reshape(1, HEAD_DIM)
    return pl.pallas_call(
        functools.partial(_rope_table_kernel, pos0=pos0, period=period),
        out_shape=(jax.ShapeDtypeStruct((n_rows, HEAD_DIM), F32),) * 2,
        name="rope_tables",
    )(inv)


def _rope(z, cos, sin):
    outs = []
    for h in range(HEADS):
        zh = z[:, h * HEAD_DIM:(h + 1) * HEAD_DIM]
        outs.append(zh * cos + pltpu.roll(zh, HEAD_DIM // 2, 1) * sin)
    return jnp.concatenate(outs, axis=-1)


def _lanes(g):
    return slice(g * HEAD_DIM, (g + 1) * HEAD_DIM)


def _store_head_rows(ref, z, tm):
    for h in range(HEADS):
        ref[pl.ds(h, tm, stride=HEADS), :] = z[:, _lanes(h)]


def _inproj_kernel(x_ref, g_ref, w_ref, gv_ref, cos_ref, sin_ref, *outs, width, prompt, n_alias):
    outs = outs[n_alias:]
    xn = (_rms(x_ref[...]) * g_ref[...]).astype(BF16)
    tm = x_ref.shape[0]

    def proj(c):
        return jnp.dot(xn, w_ref[:, c * width:(c + 1) * width], preferred_element_type=F32)

    if prompt:
        u_ref, v_ref, cx_ref, cg_ref, ko_ref, vo_ref, kb_ref, qt_ref, vt_ref, km_ref = outs
    else:
        u_ref, v_ref, cx_ref, cg_ref, q_ref, k_ref, vv_ref = outs
    u_ref[...] = jax.nn.gelu(proj(0))
    v_ref[...] = _rms(jax.nn.gelu(proj(1))) * gv_ref[...]
    cx_ref[...] = proj(5)
    cg_ref[...] = jax.nn.gelu(proj(6))
    cos = cos_ref[...]
    sin = sin_ref[...]
    q = _rope(proj(2), cos, sin)
    k = _rope(proj(3), cos, sin)
    vv = proj(4)
    if prompt:
        qt_ref[...] = q.T
        vt_ref[...] = vv.T.astype(BF16)
        kb_ref[...] = k.astype(BF16)
        for j in range(tm // MOBA_BLOCK):
            kj = k[j * MOBA_BLOCK:(j + 1) * MOBA_BLOCK]
            km_ref[j] = jnp.sum(kj, axis=0, keepdims=True) * (1.0 / MOBA_BLOCK)
        _store_head_rows(ko_ref, k, tm)
        _store_head_rows(vo_ref, vv, tm)
    else:
        q_ref[...] = q
        k_ref[...] = k
        vv_ref[...] = vv


def _inproj(x, g, w_in, l, gv, cos, sin, *, tm, table_tiles, prompt, stacks=()):
    n, d = x.shape
    width = gv.shape[1]
    depth = w_in.shape[0]
    row = lambda i: (i, 0)
    col = lambda i: (0, i)
    tab = lambda i: (i % table_tiles, 0)
    wide = (jax.ShapeDtypeStruct((n, width), F32), pl.BlockSpec((tm, width), row))
    if prompt:
        tiles = n // tm
        heads = (jax.ShapeDtypeStruct((depth * n * HEADS, HEAD_DIM), F32),
                 pl.BlockSpec((tm * HEADS, HEAD_DIM), lambda i: (l * tiles + i, 0)))
        nblk = tm // MOBA_BLOCK
        outs = [wide] * 4 + [heads] * 2 + [
            (jax.ShapeDtypeStruct((n, width), BF16), pl.BlockSpec((tm, width), row)),
            (jax.ShapeDtypeStruct((width, n), F32), pl.BlockSpec((width, tm), col)),
            (jax.ShapeDtypeStruct((width, n), BF16), pl.BlockSpec((width, tm), col)),
            (jax.ShapeDtypeStruct((n // MOBA_BLOCK, 1, width), F32),
             pl.BlockSpec((nblk, 1, width), lambda i: (i, 0, 0)))]
    else:
        outs = [wide] * 7
    return pl.pallas_call(
        functools.partial(_inproj_kernel, width=width, prompt=prompt, n_alias=len(stacks)),
        grid=(n // tm,),
        in_specs=[pl.BlockSpec((tm, d), row), _const_spec((1, d)),
                  _layer_spec((w_in.shape[0], d, 7 * width), l),
                  _const_spec((1, width)), pl.BlockSpec((tm, HEAD_DIM), tab),
                  pl.BlockSpec((tm, HEAD_DIM), tab)] + [pl.BlockSpec(memory_space=pl.ANY)] * len(stacks),
        out_specs=[o[1] for o in outs],
        out_shape=tuple(o[0] for o in outs),
        input_output_aliases={6 + j: 4 + j for j in range(len(stacks))},
        compiler_params=_cparams("parallel"),
        name="inproj_prompt" if prompt else "inproj_sample",
    )(x, g, w_in, gv, cos, sin, *stacks)


def _softplus(x):
    return jnp.maximum(x, 0.0) + jnp.log1p(jnp.exp(-jnp.abs(x)))


def _lru_coeffs(xc, g, wa_ref, ba_ref, wi_ref, bi_ref, lam_ref):
    xb = xc.astype(BF16)
    r = jax.nn.sigmoid(jnp.dot(xb, wa_ref[g], preferred_element_type=F32) + ba_ref[:, _lanes(g)])
    gate_i = jax.nn.sigmoid(jnp.dot(xb, wi_ref[g], preferred_element_type=F32) + bi_ref[:, _lanes(g)])
    log_a = (-LRU_C) * r * _softplus(-lam_ref[:, _lanes(g)])
    a = jnp.exp(log_a)
    th = jnp.tanh(log_a)
    mult = jnp.sqrt(-2.0 * th / (1.0 - th))
    return a, xc * gate_i * mult


SCAN_ROWS = SUBLANES * SUBLANES


def _mix_prompt_kernel(u_ref, v_ref, cx_ref, cg_ref, ws_ref, bias_ref, cw_ref, cb_ref,
                       wa_ref, ba_ref, wi_ref, bi_ref, lam_ref,
                       ya_ref, yc_ref, h_ref,
                       xbuf, a_scr, x_scr, hs_scr, h_scr, *, tt):
    t_idx = pl.program_id(1)

    @pl.when(t_idx == 0)
    def _():
        xbuf[0:SUBLANES, :] = jnp.zeros((SUBLANES, xbuf.shape[1]), F32)
        h_scr[...] = jnp.zeros(h_scr.shape, F32)

    r_i = lax.broadcasted_iota(jnp.int32, (CHUNK, CHUNK), 0)
    c_i = lax.broadcasted_iota(jnp.int32, (CHUNK, CHUNK), 1)
    tril = c_i <= r_i
    w_low = [jnp.where(tril, ws_ref[g], 0.0).astype(BF16) for g in range(HEADS)]
    for c in range(tt // CHUNK):
        rows = slice(c * CHUNK, (c + 1) * CHUNK)
        vc = v_ref[0, rows, :].astype(BF16)
        mixed = jnp.concatenate(
            [jnp.dot(w_low[g], vc[:, g * HEAD_DIM:(g + 1) * HEAD_DIM], preferred_element_type=F32)
             for g in range(HEADS)], axis=-1)
        ya_ref[0, rows, :] = u_ref[0, rows, :] * (mixed + bias_ref[...])

    xbuf[SUBLANES:SUBLANES + tt, :] = cx_ref[0]
    xc = cb_ref[...] + cw_ref[CONV_W - 1:CONV_W, :] * xbuf[SUBLANES:SUBLANES + tt, :]
    for d in range(1, CONV_W):
        xc = xc + cw_ref[CONV_W - 1 - d:CONV_W - d, :] * xbuf[SUBLANES - d:SUBLANES - d + tt, :]
    xbuf[0:SUBLANES, :] = xbuf[tt:tt + SUBLANES, :]

    for g in range(HEADS):
        a, xin = _lru_coeffs(xc[:, _lanes(g)], g, wa_ref, ba_ref, wi_ref, bi_ref, lam_ref)
        a_scr[g] = a
        x_scr[g] = xin
        h = h_scr[0:1, _lanes(g)]
        for s in range(tt // SCAN_ROWS):
            base = s * SCAN_ROWS
            p_run, s_run = [], []
            for j in range(SUBLANES):
                a_j = a_scr[g, pl.ds(base + j, SUBLANES, stride=SUBLANES), :]
                x_j = x_scr[g, pl.ds(base + j, SUBLANES, stride=SUBLANES), :]
                if j == 0:
                    p_run.append(a_j)
                    s_run.append(x_j)
                else:
                    p_run.append(p_run[-1] * a_j)
                    s_run.append(a_j * s_run[-1] + x_j)
            carry_in = []
            for m in range(SUBLANES):
                carry_in.append(h)
                h = s_run[-1][m:m + 1, :] + p_run[-1][m:m + 1, :] * h
            cin = jnp.concatenate(carry_in, axis=0)
            for j in range(SUBLANES):
                hs_scr[g, pl.ds(base + j, SUBLANES, stride=SUBLANES), :] = s_run[j] + p_run[j] * cin
        h_scr[:, _lanes(g)] = jnp.broadcast_to(h, (SUBLANES, HEAD_DIM))
    h_ref[0] = h_scr[...]
    yc_ref[0] = jnp.concatenate([hs_scr[g] for g in range(HEADS)], axis=-1) * cg_ref[0]


def _mix_prompt(u, v, cx, cg, ws, bias_full, cw, cb, wa, ba, wi, bi, lam, *, tt):
    b, t, w = u.shape
    seq = pl.BlockSpec((1, tt, w), lambda i, j: (i, j, 0))
    ins = [u, v, cx, cg, ws, bias_full, cw, cb, wa, ba, wi, bi, lam]
    slab = pltpu.VMEM((HEADS, tt, HEAD_DIM), F32)
    return pl.pallas_call(
        functools.partial(_mix_prompt_kernel, tt=tt),
        grid=(b, t // tt),
        in_specs=[seq] * 4 + [_const_spec(a.shape) for a in ins[4:]],
        out_specs=[seq, seq, pl.BlockSpec((1, SUBLANES, w), lambda i, j: (i, 0, 0))],
        out_shape=(jax.ShapeDtypeStruct((b, t, w), F32), jax.ShapeDtypeStruct((b, t, w), F32),
                   jax.ShapeDtypeStruct((b, SUBLANES, w), F32)),
        scratch_shapes=[pltpu.VMEM((tt + SUBLANES, w), F32), slab, slab, slab,
                        pltpu.VMEM((SUBLANES, w), F32)],
        compiler_params=_cparams("arbitrary", "arbitrary"),
        name="mix_prompt",
    )(*ins)


def _mix_sample_kernel(u_ref, v_ref, cx_ref, cg_ref, wrow_ref, brow_ref, cw_ref, cb_ref,
                       wa_ref, ba_ref, wi_ref, bi_ref, lam_ref, h0_ref, conv0_ref,
                       ya_ref, yc_ref, h_ref, in_scr, out_scr, *, steps, nseq):
    w = u_ref.shape[1]
    for n, ref in enumerate((u_ref, v_ref, cx_ref, cg_ref)):
        for g in range(HEADS):
            in_scr[n * HEADS + g] = ref[:, _lanes(g)]

    def at_step(n, g, t):
        return in_scr[n * HEADS + g, pl.ds(t, nseq, stride=steps), :]

    for g in range(HEADS):
        v_t = [at_step(1, g, t) for t in range(steps)]
        for t in range(steps):
            mixed = brow_ref[t:t + 1, _lanes(g)]
            for s in range(t + 1):
                mixed = mixed + wrow_ref[t * steps + s:t * steps + s + 1, _lanes(g)] * v_t[s]
            out_scr[g, pl.ds(t, nseq, stride=steps), :] = at_step(0, g, t) * mixed

        xp = [conv0_ref[:, j * w + g * HEAD_DIM:j * w + (g + 1) * HEAD_DIM] for j in range(CONV_W - 1)]
        xp += [at_step(2, g, t) for t in range(steps)]
        h = h0_ref[:, _lanes(g)]
        for t in range(steps):
            xc = cb_ref[:, _lanes(g)]
            for j in range(CONV_W):
                xc = xc + cw_ref[j:j + 1, _lanes(g)] * xp[t + j]
            a, xin = _lru_coeffs(xc, g, wa_ref, ba_ref, wi_ref, bi_ref, lam_ref)
            h = a * h + xin
            out_scr[HEADS + g, pl.ds(t, nseq, stride=steps), :] = h * at_step(3, g, t)
        h_ref[:, _lanes(g)] = h
    ya_ref[...] = jnp.concatenate([out_scr[g] for g in range(HEADS)], axis=-1)
    yc_ref[...] = jnp.concatenate([out_scr[HEADS + g] for g in range(HEADS)], axis=-1)


def _mix_sample(u, v, cx, cg, wrow, brow, cw, cb, wa, ba, wi, bi, lam, h0, conv0, *, steps):
    n, w = u.shape
    nseq = n // steps
    ins = [u, v, cx, cg, wrow, brow, cw, cb, wa, ba, wi, bi, lam, h0, conv0]
    return pl.pallas_call(
        functools.partial(_mix_sample_kernel, steps=steps, nseq=nseq),
        grid=(1,),
        in_specs=[_const_spec(a.shape) for a in ins],
        out_specs=[_const_spec((n, w)), _const_spec((n, w)), _const_spec((nseq, w))],
        out_shape=(jax.ShapeDtypeStruct((n, w), F32), jax.ShapeDtypeStruct((n, w), F32),
                   jax.ShapeDtypeStruct((nseq, w), F32)),
        scratch_shapes=[pltpu.VMEM((4 * HEADS, n, HEAD_DIM), F32),
                        pltpu.VMEM((2 * HEADS, n, HEAD_DIM), F32)],
        compiler_params=_cparams("arbitrary"),
        name="mix_sample",
    )(*ins)


def _topk_mask(g, idx, n, axis):
    sel = jnp.zeros(g.shape, jnp.bool_)
    for _ in range(MOBA_TOPK):
        best = jnp.max(g, axis=axis, keepdims=True)
        first = jnp.min(jnp.where(g == best, idx, n), axis=axis, keepdims=True)
        pick = idx == first
        sel = jnp.logical_or(sel, pick)
        g = jnp.where(pick, -jnp.inf, g)
    return sel


SUM_ROWS = 16


QUERY_BLOCKS = 2


def _attn_prompt_kernel(qt_ref, kb_ref, vt_ref, km_ref, o_ref, bias_scr, *, nb, group):
    blk = MOBA_BLOCK
    span = group * blk
    nq = QUERY_BLOCKS * blk
    i0 = pl.program_id(1) * QUERY_BLOCKS
    own = pl.multiple_of(i0 * blk, nq)
    blk_i = lax.broadcasted_iota(jnp.int32, (nb, nq), 0)
    lane_i = lax.broadcasted_iota(jnp.int32, (nb, nq), 1)
    past = blk_i < i0 + lane_i // blk
    key_i = lax.broadcasted_iota(jnp.int32, (nq, nq), 0)
    qry_i = lax.broadcasted_iota(jnp.int32, (nq, nq), 1)
    routed_own = jnp.logical_and(key_i < blk, qry_i >= blk)

    def values(h, start, size):
        ones = jnp.ones((SUM_ROWS, size), BF16)
        return jnp.concatenate([vt_ref[_lanes(h), pl.ds(start, size)], ones], axis=0)

    qs, own_bias = [], []
    for h in range(HEADS):
        qt = qt_ref[_lanes(h), :]
        gate = jnp.dot(km_ref[0, :, _lanes(h)], qt, precision=lax.Precision.HIGHEST,
                       preferred_element_type=F32)
        sel = jnp.logical_and(_topk_mask(jnp.where(past, gate, -jnp.inf), blk_i, nb, 0), past)
        sel_bias = jnp.where(sel, 0.0, NEG)
        own_bias.append(jnp.sum(jnp.where(blk_i == i0, sel_bias, 0.0), axis=0, keepdims=True))
        loop_bias = jnp.where(blk_i < i0, sel_bias, NEG)
        for sj in range(nb // group):
            bias_scr[h, sj, 0:group, :] = loop_bias[sj * group:(sj + 1) * group, :]
        qs.append((qt * (HEAD_DIM ** -0.5 * LOG2_E)).astype(BF16))

    def logits(h, start, size):
        return jnp.dot(kb_ref[0, pl.ds(start, size), _lanes(h)], qs[h], preferred_element_type=F32)

    m0, acc0 = [], []
    s_next = logits(0, own, nq)
    for h in range(HEADS):
        s = jnp.where(key_i <= qry_i, s_next + jnp.where(routed_own, own_bias[h], 0.0), NEG)
        if h + 1 < HEADS:
            s_next = logits(h + 1, own, nq)
        m0.append(jnp.max(s, axis=0, keepdims=True))
        acc0.append(jnp.dot(values(h, own, nq), jnp.exp2(s - m0[h]).astype(BF16),
                            preferred_element_type=F32))

    def body(sj, carry):
        start = pl.multiple_of(sj * span, span)

        out = []
        s_next = logits(0, start, span)
        for h in range(HEADS):
            m, acc = carry[h]
            s = s_next
            if h + 1 < HEADS:
                s_next = logits(h + 1, start, span)
            bias = bias_scr[h, sj, 0:group, :]
            s = (s.reshape(group, blk, nq) + bias[:, None, :]).reshape(span, nq)
            m_new = jnp.maximum(m, jnp.max(s, axis=0, keepdims=True))
            p = jnp.exp2(s - m_new).astype(BF16)
            acc = jnp.exp2(m - m_new) * acc + jnp.dot(values(h, start, span), p,
                                                      preferred_element_type=F32)
            out.append((m_new, acc))
        return tuple(out)

    fin = lax.fori_loop(0, (i0 + group - 1) // group, body, tuple(zip(m0, acc0)))
    o_ref[0] = jnp.concatenate(
        [(acc[:HEAD_DIM] / acc[HEAD_DIM:HEAD_DIM + 1]).T for _, acc in fin], axis=-1)


def _attn_prompt(qt, kb, vt, kmean):
    b, t, w = kb.shape
    nb = t // MOBA_BLOCK
    nq = QUERY_BLOCKS * MOBA_BLOCK
    steps = nb // QUERY_BLOCKS
    group = 4 if nb % 4 == 0 else QUERY_BLOCKS
    whole = dict(pipeline_mode=pl.Buffered(1))
    return pl.pallas_call(
        functools.partial(_attn_prompt_kernel, nb=nb, group=group),
        grid=(b, steps),
        in_specs=[pl.BlockSpec((w, nq), lambda bi, i: (0, bi * steps + i)),
                  pl.BlockSpec((1, t, w), lambda bi, i: (bi, 0, 0), **whole),
                  pl.BlockSpec((w, t), lambda bi, i: (0, bi), **whole),
                  pl.BlockSpec((1, nb, w), lambda bi, i: (bi, 0, 0), **whole)],
        out_specs=pl.BlockSpec((1, nq, w), lambda bi, i: (bi, i, 0)),
        out_shape=jax.ShapeDtypeStruct((b, t, w), F32),
        scratch_shapes=[pltpu.VMEM((HEADS, nb // group, SUBLANES, nq), F32)],
        compiler_params=_cparams("arbitrary", "arbitrary"),
        name="attn_prompt",
    )(qt, kb, vt, kmean)


def _attn_sample_kernel(pt_ref, q_ref, kn_ref, vn_ref, *rest, n_pages, page, steps):
    del pt_ref
    kp = rest[:n_pages]
    vp = rest[n_pages:2 * n_pages]
    o_ref, kbuf, vbuf = rest[2 * n_pages:]
    past = n_pages * page
    nbp = past // MOBA_BLOCK
    per_blk = MOBA_BLOCK // page
    cols = LANES

    w = HEADS * HEAD_DIM
    sums = []
    for p in range(n_pages):
        kpg = jnp.concatenate([kp[p][0, 0, pl.ds(h, page, stride=HEADS), :] for h in range(HEADS)], axis=-1)
        vpg = jnp.concatenate([vp[p][0, 0, pl.ds(h, page, stride=HEADS), :] for h in range(HEADS)], axis=-1)
        sums.append(jnp.sum(kpg, axis=0, keepdims=True))
        kbuf[p * page:(p + 1) * page, :] = kpg.astype(BF16)
        vbuf[p * page:(p + 1) * page, :] = vpg.astype(BF16)
    pad = jnp.zeros((LANES - steps, w), F32)
    kbuf[past:past + LANES, :] = jnp.concatenate([kn_ref[0], pad], axis=0).astype(BF16)
    vbuf[past:past + LANES, :] = jnp.concatenate([vn_ref[0], pad], axis=0).astype(BF16)
    kmean = jnp.concatenate(
        [sum(sums[j * per_blk:(j + 1) * per_blk]) for j in range(nbp)], axis=0) * (1.0 / MOBA_BLOCK)

    q = q_ref[0]
    r_i = lax.broadcasted_iota(jnp.int32, (cols, w), 0)
    l_i = lax.broadcasted_iota(jnp.int32, (cols, w), 1)
    qmat = jnp.where(r_i // steps == l_i // HEAD_DIM, jnp.concatenate([q] * (cols // steps), axis=0), 0.0)
    gate = lax.dot_general(kmean, qmat, _NT, precision=lax.Precision.HIGHEST,
                           preferred_element_type=F32)
    s = lax.dot_general(kbuf[...], (qmat * (HEAD_DIM ** -0.5)).astype(BF16), _NT,
                        preferred_element_type=F32)

    blk_i = lax.broadcasted_iota(jnp.int32, (nbp, cols), 0)
    sel_bias = jnp.where(_topk_mask(gate, blk_i, nbp, 0), 0.0, NEG)
    s_past = (s[:past].reshape(nbp, MOBA_BLOCK, cols) + sel_bias[:, None, :]).reshape(past, cols)
    key_i = lax.broadcasted_iota(jnp.int32, (LANES, cols), 0)
    col_i = lax.broadcasted_iota(jnp.int32, (LANES, cols), 1)
    s_own = jnp.where(key_i <= col_i % steps, s[past:], NEG)
    m = jnp.maximum(jnp.max(s_past, axis=0, keepdims=True), jnp.max(s_own, axis=0, keepdims=True))
    p_t = jnp.concatenate([jnp.exp(s_past - m), jnp.exp(s_own - m)], axis=0).T
    l = jnp.sum(p_t, axis=1, keepdims=True)
    out = jnp.dot(p_t.astype(BF16), vbuf[...], preferred_element_type=F32) / l
    o_ref[0] = jnp.concatenate(
        [out[h * steps:(h + 1) * steps, _lanes(h)] for h in range(HEADS)], axis=-1)


def _attn_sample(page_table, q, k_new, v_new, cache_k, cache_v, layer):
    bs, steps, w = q.shape
    n_pages = page_table.shape[1]
    page = cache_k.shape[2] // HEADS
    new = pl.BlockSpec((1, steps, w), lambda b, pt: (b, 0, 0))

    def page_spec(p):
        return pl.BlockSpec((1, 1, page * HEADS, HEAD_DIM), lambda b, pt: (layer, pt[b, p], 0, 0))

    keys = n_pages * page + LANES
    return pl.pallas_call(
        functools.partial(_attn_sample_kernel, n_pages=n_pages, page=page, steps=steps),
        grid_spec=pltpu.PrefetchScalarGridSpec(
            num_scalar_prefetch=1,
            grid=(bs,),
            in_specs=[new, new, new] + [page_spec(p) for p in range(n_pages)] * 2,
            out_specs=new,
            scratch_shapes=[pltpu.VMEM((keys, w), BF16), pltpu.VMEM((keys, w), BF16)],
        ),
        out_shape=jax.ShapeDtypeStruct((bs, steps, w), F32),
        compiler_params=_cparams("arbitrary"),
        name="attn_sample",
    )(page_table, q, k_new, v_new, *([cache_k] * n_pages), *([cache_v] * n_pages))


def _merge_kernel(x_ref, ya_ref, yb_ref, yc_ref, gpre_ref, wg_ref, wb_ref, wo_ref, gpost_ref,
                  o_ref, *, d, col0):
    x = x_ref[...]
    xn = (_rms(x) * gpre_ref[...]).astype(BF16)
    merged = None
    for n, y_ref in enumerate((ya_ref, yb_ref, yc_ref)):
        gate = jax.nn.sigmoid(jnp.dot(xn, wg_ref[:, col0 + n * d:col0 + (n + 1) * d], preferred_element_type=F32))
        proj = jnp.dot(y_ref[...].astype(BF16), wb_ref[n], preferred_element_type=F32)
        merged = gate * proj if merged is None else merged + gate * proj
    mix = jnp.dot(merged.astype(BF16), wo_ref[...], preferred_element_type=F32)
    o_ref[...] = x + _rms(mix) * gpost_ref[...]


def _merge(x, ya, yb, yc, gpre, wg, wb, wo, l, gpost, *, tm):
    n, d = x.shape
    w = ya.shape[1]
    row = lambda i: (i, 0)
    return pl.pallas_call(
        functools.partial(_merge_kernel, d=d, col0=wg.shape[2] - 3 * d),
        grid=(n // tm,),
        in_specs=[pl.BlockSpec((tm, d), row)] + [pl.BlockSpec((tm, w), row)] * 3
                 + [_const_spec(gpre.shape)] + [_layer_spec(a.shape, l) for a in (wg, wb, wo)]
                 + [_const_spec(gpost.shape)],
        out_specs=pl.BlockSpec((tm, d), row),
        out_shape=jax.ShapeDtypeStruct((n, d), F32),
        compiler_params=_cparams("parallel"),
        name="merge",
    )(x, ya, yb, yc, gpre, wg, wb, wo, gpost)


def _ffn_kernel(x_ref, gpre_ref, wgu_ref, wd_ref, gpost_ref, o_ref, *, d_ff):
    x = x_ref[...]
    xn = (_rms(x) * gpre_ref[...]).astype(BF16)
    g = jnp.dot(xn, wgu_ref[:, :d_ff], preferred_element_type=F32)
    u = jnp.dot(xn, wgu_ref[:, d_ff:], preferred_element_type=F32)
    act = (jax.nn.silu(g) * u).astype(BF16)
    f = jnp.dot(act, wd_ref[...], preferred_element_type=F32)
    o_ref[...] = x + _rms(f) * gpost_ref[...]


def _ffn(x, gpre, wgu, wd, l, gpost, *, tm):
    n, d = x.shape
    row = lambda i: (i, 0)
    return pl.pallas_call(
        functools.partial(_ffn_kernel, d_ff=wd.shape[1]),
        grid=(n // tm,),
        in_specs=[pl.BlockSpec((tm, d), row), _const_spec(gpre.shape), _layer_spec(wgu.shape, l),
                  _layer_spec(wd.shape, l), _const_spec(gpost.shape)],
        out_specs=pl.BlockSpec((tm, d), row),
        out_shape=jax.ShapeDtypeStruct((n, d), F32),
        compiler_params=_cparams("parallel"),
        name="ffn",
    )(x, gpre, wgu, wd, gpost)


def _row_tile(n):
    for tm in (512, 256, 128, 64, 32, 16, 8):
        if n % tm == 0:
            return tm
    raise ValueError(f"row count {n} is not a multiple of {SUBLANES}")


def _layer_weights(l, P):
    row = lambda a: a[l].reshape(1, -1)
    return dict(
        g_mix_pre=row(P['norm_mix_pre']), g_mix_post=row(P['norm_mix_post']),
        g_ffn_pre=row(P['norm_ffn_pre']), g_ffn_post=row(P['norm_ffn_post']),
        l=l, w_in=P['w_in'], gv=row(P['a_norm_v']),
        ws=P['a_w_s'][l], bs=P['a_b_s'][l],
        cw=P['c_conv_w'][l], cb=row(P['c_conv_b']),
        wa=P['c_w_a'][l].astype(BF16), ba=row(P['c_b_a']),
        wi=P['c_w_i'][l].astype(BF16), bi=row(P['c_b_i']), lam=row(P['c_lambda']),
        wb=P['w_branch'], wo=P['w_out'], wgu=P['ffn_w_gu'], wd=P['ffn_w_down'])


def _finish_layer(x, ya, yb, yc, L, tm):
    x1 = _merge(x, ya, yb, yc, L['g_mix_pre'], L['w_in'], L['wb'], L['wo'], L['l'], L['g_mix_post'], tm=tm)
    return _ffn(x1, L['g_ffn_pre'], L['wgu'], L['wd'], L['l'], L['g_ffn_post'], tm=tm)


def _prompt_layer(x, L, cos, sin, stacks):
    b, t, d = x.shape
    width = L['gv'].shape[1]
    tm = _row_tile(t)
    xf = x.reshape(b * t, d)
    u, v, cx, cg, k_rows, v_rows, kb, qt, vt, kmean = _inproj(
        xf, L['g_mix_pre'], L['w_in'], L['l'], L['gv'], cos, sin, tm=tm, table_tiles=t // tm, prompt=True,
        stacks=stacks)
    seq = lambda a: a.reshape(b, t, width)
    bias_full = jnp.repeat(L['bs'][:, :CHUNK].T, HEAD_DIM, axis=1)
    ya, yc, h_last = _mix_prompt(seq(u), seq(v), seq(cx), seq(cg), L['ws'][:, :CHUNK, :CHUNK], bias_full,
                                 L['cw'], L['cb'], L['wa'], L['ba'], L['wi'], L['bi'], L['lam'],
                                 tt=_row_tile(t))
    yb = _attn_prompt(qt, seq(kb), vt, kmean.reshape(b, t // MOBA_BLOCK, width))
    y = _finish_layer(xf, ya.reshape(b * t, width), yb.reshape(b * t, width),
                      yc.reshape(b * t, width), L, tm)
    return (y.reshape(b, t, d), k_rows, v_rows, h_last[:, 0],
            seq(cx)[:, t - (CONV_W - 1):], seq(v)[:, t - CHUNK:])


def _sample_layer(l, x, L, cos, sin, page_table, cache_k, cache_v, h0, conv0):
    bs, steps, d = x.shape
    width = L['gv'].shape[1]
    n = bs * steps
    tm = _row_tile(n)
    xf = x.reshape(n, d)
    u, v, cx, cg, q, k, vv = _inproj(xf, L['g_mix_pre'], L['w_in'], L['l'], L['gv'], cos, sin,
                                     tm=tm, table_tiles=1, prompt=False)
    w_low = jnp.tril(L['ws'][:, :steps, :steps])
    wrow = jnp.repeat(w_low.transpose(1, 2, 0).reshape(steps * steps, HEADS), HEAD_DIM, axis=1)
    brow = jnp.repeat(L['bs'][:, :steps].T, HEAD_DIM, axis=1)
    ya, yc, h_new = _mix_sample(u, v, cx, cg, wrow, brow, L['cw'], L['cb'], L['wa'], L['ba'],
                                L['wi'], L['bi'], L['lam'], h0, conv0.reshape(bs, -1), steps=steps)
    seq = lambda a: a.reshape(bs, steps, width)
    yb = _attn_sample(page_table, seq(q), seq(k), seq(vv), cache_k, cache_v, l)
    y = _finish_layer(xf, ya, yb.reshape(n, width), yc, L, tm)
    conv_new = jnp.concatenate([conv0, seq(cx)], axis=1)[:, -(CONV_W - 1):]
    heads = lambda a: a.reshape(bs, steps, HEADS, HEAD_DIM)
    return y.reshape(bs, steps, d), heads(k), heads(vv), h_new, conv_new, seq(v)


def kernel(x_prompt, x_sample, cache_k, cache_v, page_table, state_lru_h, state_conv, norm_mix_pre, norm_mix_post, norm_ffn_pre, norm_ffn_post, w_in, a_norm_v, a_w_s, a_b_s, c_conv_w, c_conv_b, c_w_a, c_b_a, c_w_i, c_b_i, c_lambda, w_branch, w_out, ffn_w_gu, ffn_w_down):
    P = dict(norm_mix_pre=norm_mix_pre, norm_mix_post=norm_mix_post, norm_ffn_pre=norm_ffn_pre,
             norm_ffn_post=norm_ffn_post, w_in=w_in, a_norm_v=a_norm_v, a_w_s=a_w_s, a_b_s=a_b_s,
             c_conv_w=c_conv_w, c_conv_b=c_conv_b, c_w_a=c_w_a, c_b_a=c_b_a, c_w_i=c_w_i,
             c_b_i=c_b_i, c_lambda=c_lambda, w_branch=w_branch, w_out=w_out,
             ffn_w_gu=ffn_w_gu, ffn_w_down=ffn_w_down)
    for name in ('w_in', 'w_branch', 'w_out', 'ffn_w_gu', 'ffn_w_down'):
        P[name] = P[name].astype(BF16)
    depth = w_in.shape[0]
    tp = x_prompt.shape[1]
    bs, steps, _ = x_sample.shape
    n_pages, page = page_table.shape[1], cache_k.shape[2]
    past = n_pages * page
    assert past % MOBA_BLOCK == 0 and MOBA_BLOCK % page == 0 and tp % (QUERY_BLOCKS * MOBA_BLOCK) == 0
    assert cache_k.shape[3:] == (HEADS, HEAD_DIM)
    ck = cache_k.reshape(cache_k.shape[0], cache_k.shape[1], page * HEADS, HEAD_DIM)
    cv = cache_v.reshape(ck.shape)

    cos_p, sin_p = _rope_tables(tp, 0, tp)
    cos_s, sin_s = _rope_tables(_row_tile(bs * steps), past, steps)

    yp, ys = x_prompt, x_sample
    outs = [[] for _ in range(10)]
    kv_stacks = ()
    for l in range(depth):
        L = _layer_weights(l, P)
        yp, k1, v1, h1, c1, a1 = _prompt_layer(yp, L, cos_p, sin_p, kv_stacks)
        kv_stacks = (k1, v1)
        ys, k2, v2, h2, c2, a2 = _sample_layer(l, ys, L, cos_s, sin_s, page_table, ck, cv,
                                               state_lru_h[l], state_conv[l])
        for lst, val in zip(outs[2:], (k2, v2, h1, h2, c1, c2, a1, a2)):
            lst.append(val)
    bp = x_prompt.shape[0]
    kv_prompt = tuple(a.reshape(depth, bp, tp, HEADS, HEAD_DIM) for a in kv_stacks)
    return (yp, ys) + kv_prompt + tuple(jnp.stack(o) for o in outs[2:])
```
